```python
import math, functools
import jax, jax.numpy as jnp
from jax import lax
import numpy as np

D_MODEL = 1024
BATCH = 8
SEQ = 2048
DEPTH = 2
DEC_BATCH = 128
DEC_SEQ = 4
PAST_LEN = 2048
PAGE_SIZE = 128

MIX_WIDTH = D_MODEL
RWKV_WIDTH = MIX_WIDTH // 2
RWKV_HEAD_DIM = 64
RWKV_HEADS = RWKV_WIDTH // RWKV_HEAD_DIM
DECAY_RANK = 64
ICLR_RANK = 64
DIFF_WIDTH = MIX_WIDTH - RWKV_WIDTH
DIFF_HEAD_DIM = 64
DIFF_K_DIM = 2 * DIFF_HEAD_DIM
DIFF_V_DIM = 2 * DIFF_HEAD_DIM
DIFF_HEADS = DIFF_WIDTH // DIFF_V_DIM
DIFF_QK = DIFF_HEADS * DIFF_K_DIM
SHIFT_COLS = 3 * RWKV_WIDTH + DECAY_RANK + ICLR_RANK
IN_COLS = SHIFT_COLS + RWKV_WIDTH + 2 * DIFF_QK + DIFF_WIDTH + DIFF_WIDTH
Q_BLOCK = 128
RMS_EPS = 1e-6
SUBLN_EPS = 1e-5
GN_EPS = 64e-5

kernel_name = "hymba_rwkv7_diffattn_step"


def rms_norm(x, w, eps):
    xf = x.astype(jnp.float32)
    y = xf * lax.rsqrt(jnp.mean(xf * xf, axis=-1, keepdims=True) + eps)
    return y * w.astype(jnp.float32)


def rwkv7_scan(S0, r, decay, k, v, a_vec, b_vec):
    def step(S, inp):
        r_t, w_t, k_t, v_t, a_t, b_t = inp
        sa = jnp.einsum('bhvk,bhk->bhv', S, a_t)
        S = S * w_t[:, :, None, :] + sa[..., None] * b_t[:, :, None, :] + v_t[..., None] * k_t[:, :, None, :]
        return S, jnp.einsum('bhvk,bhk->bhv', S, r_t)
    xs = tuple(jnp.swapaxes(t, 0, 1) for t in (r, decay, k, v, a_vec, b_vec))
    S, ys = lax.scan(step, S0, xs)
    return jnp.swapaxes(ys, 0, 1), S


def rwkv7_branch(cols, gate, prev_row, S0, shift_mu, decay_w0, decay_lora_b, iclr_a0, iclr_lora_b,
                 key_k, key_a, bonus_r_k, lnx_w, lnx_b):
    f32 = jnp.float32
    B, T, _ = cols.shape
    c = cols.astype(f32)
    shifted = jnp.concatenate([prev_row.astype(f32)[:, None, :], c[:, :-1, :]], axis=1)
    xs = c + (shifted - c) * shift_mu.astype(f32)
    W = RWKV_WIDTH
    r, k, v, xw, xa = jnp.split(xs, [W, 2 * W, 3 * W, 3 * W + DECAY_RANK], axis=-1)
    w = -jax.nn.softplus(-(decay_w0.astype(f32) + jnp.tanh(xw) @ decay_lora_b.astype(f32))) - 0.5
    decay = jnp.exp(-jnp.exp(w))
    a = jax.nn.sigmoid(iclr_a0.astype(f32) + xa @ iclr_lora_b.astype(f32))
    heads = lambda t: t.reshape(B, T, RWKV_HEADS, RWKV_HEAD_DIM)
    kk = heads(k * key_k.astype(f32))
    kk = kk / jnp.maximum(jnp.sqrt(jnp.sum(kk * kk, axis=-1, keepdims=True)), 1e-12)
    k = k * (1.0 + (a - 1.0) * key_a.astype(f32))
    r_h, k_h, v_h, d_h, a_h = heads(r), heads(k), heads(v), heads(decay), heads(a)
    y, S = rwkv7_scan(S0.astype(f32), r_h, d_h, k_h, v_h, -kk, kk * a_h)
    mu = jnp.mean(y, axis=-1, keepdims=True)
    var = jnp.mean(jnp.square(y - mu), axis=-1, keepdims=True)
    y = ((y - mu) * lax.rsqrt(var + GN_EPS)).reshape(B, T, W) * lnx_w.astype(f32) + lnx_b.astype(f32)
    bonus = jnp.sum(r_h * k_h * bonus_r_k.astype(f32), axis=-1, keepdims=True) * v_h
    y = y + bonus.reshape(B, T, W)
    return y * jax.nn.silu(gate.astype(f32)), S


def diff_lambda(lam_q1, lam_k1, lam_q2, lam_k2, lam_init):
    f = lambda p, q: jnp.exp(jnp.sum(p.astype(jnp.float32) * q.astype(jnp.float32)))
    return f(lam_q1, lam_k1) - f(lam_q2, lam_k2) + lam_init


def diff_attend(q, k, v, q_pos, k_pos, lam):
    s = jnp.einsum('bqhcd,bkhcd->bhcqk', q.astype(jnp.float32), k.astype(jnp.float32)) / math.sqrt(DIFF_HEAD_DIM)
    mask = k_pos[None, :] <= q_pos[:, None]
    s = jnp.where(mask, s, -jnp.inf)
    p = jax.nn.softmax(s, axis=-1)
    attn = p[:, :, 0] - lam * p[:, :, 1]
    return jnp.einsum('bhqk,bkhe->bqhe', attn, v.astype(jnp.float32))


def diff_prompt(q, k, v, lam):
    B, T = q.shape[:2]
    qb = min(Q_BLOCK, T)
    nb = T // qb
    q_blocks = jnp.swapaxes(q.reshape(B, nb, qb, DIFF_HEADS, 2, DIFF_HEAD_DIM), 0, 1)
    k_pos = jnp.arange(T)

    def one(args):
        q_blk, i = args
        return diff_attend(q_blk, k, v, i * qb + jnp.arange(qb), k_pos, lam)

    o = lax.map(one, (q_blocks, jnp.arange(nb)))
    return jnp.swapaxes(o, 0, 1).reshape(B, T, DIFF_HEADS, DIFF_V_DIM)


def diff_sample(q, k, v, lam, k_pool, v_pool, page_table):
    B, T = q.shape[:2]
    k_past = k_pool[page_table].reshape(B, -1, DIFF_HEADS, 2, DIFF_HEAD_DIM).astype(k.dtype)
    v_past = v_pool[page_table].reshape(B, -1, DIFF_HEADS, DIFF_V_DIM).astype(v.dtype)
    past = k_past.shape[1]
    k_all = jnp.concatenate([k_past, k], axis=1)
    v_all = jnp.concatenate([v_past, v], axis=1)
    return diff_attend(q, k_all, v_all, past + jnp.arange(T), jnp.arange(past + T), lam)


def mixer_layer(x, S0, prev_row, attend, lam_init, norm_w, w_in, shift_mu, decay_w0, decay_lora_b,
                iclr_a0, iclr_lora_b, key_k, key_a, bonus_r_k, lnx_w, lnx_b,
                lam_q1, lam_k1, lam_q2, lam_k2, subln_w, w_out):
    f32 = jnp.float32
    B, T, _ = x.shape
    h = rms_norm(x, norm_w, RMS_EPS).astype(x.dtype)
    proj = h @ w_in
    c0 = SHIFT_COLS
    c1 = c0 + RWKV_WIDTH
    c2 = c1 + DIFF_QK
    c3 = c2 + DIFF_QK
    c4 = c3 + DIFF_WIDTH
    rwkv_cols, rwkv_gate = proj[..., :c0], proj[..., c0:c1]
    q = proj[..., c1:c2].reshape(B, T, DIFF_HEADS, 2, DIFF_HEAD_DIM)
    k = proj[..., c2:c3].reshape(B, T, DIFF_HEADS, 2, DIFF_HEAD_DIM)
    v = proj[..., c3:c4].reshape(B, T, DIFF_HEADS, DIFF_V_DIM)
    diff_gate = proj[..., c4:]
    rwkv_out, S_new = rwkv7_branch(rwkv_cols, rwkv_gate, prev_row, S0, shift_mu, decay_w0, decay_lora_b,
                                   iclr_a0, iclr_lora_b, key_k, key_a, bonus_r_k, lnx_w, lnx_b)
    lam = diff_lambda(lam_q1, lam_k1, lam_q2, lam_k2, lam_init)
    o = attend(q, k, v, lam)
    o = rms_norm(o, subln_w, SUBLN_EPS) * (1.0 - lam_init)
    diff_out = o.reshape(B, T, DIFF_WIDTH) * jax.nn.silu(diff_gate.astype(f32))
    mixed = jnp.concatenate([rwkv_out, diff_out], axis=-1).astype(x.dtype)
    x_new = x + mixed @ w_out
    return x_new, S_new, rwkv_cols[:, -1, :], k.reshape(B, T, DIFF_HEADS, DIFF_K_DIM), v


def setup_inputs(seed: int = 0) -> dict:
    key = jax.random.key(seed)
    ks = jax.random.split(key, 32)
    f32 = jnp.float32
    nrm = lambda kk, shape, s=1.0: s * jax.random.normal(kk, shape, f32)
    n_pages = PAST_LEN // PAGE_SIZE
    n_used = DEC_BATCH * n_pages
    n_phys = n_used + max(1, n_used // 4)
    perm = jax.random.permutation(ks[4], n_phys)
    page_table = perm[:n_used].reshape(DEC_BATCH, n_pages).astype(jnp.int32)
    return {
        'x_prompt': nrm(ks[0], (BATCH, SEQ, D_MODEL)),
        'x_sample': nrm(ks[1], (DEC_BATCH, DEC_SEQ, D_MODEL)),
        'cache_k': nrm(ks[2], (DEPTH, n_phys, PAGE_SIZE, DIFF_HEADS, DIFF_K_DIM)),
        'cache_v': nrm(ks[3], (DEPTH, n_phys, PAGE_SIZE, DIFF_HEADS, DIFF_V_DIM)),
        'page_table': page_table,
        'state_rwkv': nrm(ks[5], (DEPTH, DEC_BATCH, RWKV_HEADS, RWKV_HEAD_DIM, RWKV_HEAD_DIM), 0.3),
        'state_shift': nrm(ks[6], (DEPTH, DEC_BATCH, SHIFT_COLS)),
        'norm_w': 1.0 + nrm(ks[7], (DEPTH, D_MODEL), 0.02),
        'w_in': nrm(ks[8], (DEPTH, D_MODEL, IN_COLS), D_MODEL ** -0.5),
        'shift_mu': jax.random.uniform(ks[9], (DEPTH, SHIFT_COLS), f32),
        'decay_w0': -3.0 + nrm(ks[10], (DEPTH, RWKV_WIDTH)),
        'decay_lora_b': nrm(ks[11], (DEPTH, DECAY_RANK, RWKV_WIDTH), 0.1 * DECAY_RANK ** -0.5),
        'iclr_a0': nrm(ks[12], (DEPTH, RWKV_WIDTH), 0.1),
        'iclr_lora_b': nrm(ks[13], (DEPTH, ICLR_RANK, RWKV_WIDTH), 0.5 * ICLR_RANK ** -0.5),
        'key_k': 0.85 + nrm(ks[14], (DEPTH, RWKV_WIDTH), 0.02),
        'key_a': 1.0 + nrm(ks[15], (DEPTH, RWKV_WIDTH), 0.02),
        'bonus_r_k': nrm(ks[16], (DEPTH, RWKV_HEADS, RWKV_HEAD_DIM), 0.1),
        'lnx_w': 1.0 + nrm(ks[17], (DEPTH, RWKV_WIDTH), 0.02),
        'lnx_b': nrm(ks[18], (DEPTH, RWKV_WIDTH), 0.02),
        'lam_q1': nrm(ks[19], (DEPTH, DIFF_HEAD_DIM), 0.1),
        'lam_k1': nrm(ks[20], (DEPTH, DIFF_HEAD_DIM), 0.1),
        'lam_q2': nrm(ks[21], (DEPTH, DIFF_HEAD_DIM), 0.1),
        'lam_k2': nrm(ks[22], (DEPTH, DIFF_HEAD_DIM), 0.1),
        'subln_w': 1.0 + nrm(ks[23], (DEPTH, DIFF_V_DIM), 0.02),
        'w_out': nrm(ks[24], (DEPTH, MIX_WIDTH, D_MODEL), MIX_WIDTH ** -0.5),
        'final_norm_w': 1.0 + nrm(ks[25], (D_MODEL,), 0.02),
    }


def reference(x_prompt, x_sample, cache_k, cache_v, page_table, state_rwkv, state_shift,
              norm_w, w_in, shift_mu, decay_w0, decay_lora_b, iclr_a0, iclr_lora_b, key_k, key_a,
              bonus_r_k, lnx_w, lnx_b, lam_q1, lam_k1, lam_q2, lam_k2, subln_w, w_out, final_norm_w):
    xp, xs = x_prompt, x_sample
    Bp = x_prompt.shape[0]
    kp_l, vp_l, sp_l, hp_l = [], [], [], []
    ks_l, vs_l, ss_l, hs_l = [], [], [], []
    for l in range(DEPTH):
        lam_init = 0.8 - 0.6 * math.exp(-0.3 * l)
        lp = (norm_w[l], w_in[l], shift_mu[l], decay_w0[l], decay_lora_b[l], iclr_a0[l], iclr_lora_b[l],
              key_k[l], key_a[l], bonus_r_k[l], lnx_w[l], lnx_b[l],
              lam_q1[l], lam_k1[l], lam_q2[l], lam_k2[l], subln_w[l], w_out[l])
        S0p = jnp.zeros((Bp, RWKV_HEADS, RWKV_HEAD_DIM, RWKV_HEAD_DIM), jnp.float32)
        prev0p = jnp.zeros((Bp, SHIFT_COLS), xp.dtype)
        xp, Sp, hp, kp, vp = mixer_layer(xp, S0p, prev0p, diff_prompt, lam_init, *lp)
        attend_s = functools.partial(diff_sample, k_pool=cache_k[l], v_pool=cache_v[l], page_table=page_table)
        xs, Ss, hs, kss, vss = mixer_layer(xs, state_rwkv[l], state_shift[l], attend_s, lam_init, *lp)
        kp_l.append(kp)
        vp_l.append(vp)
        sp_l.append(Sp.astype(x_prompt.dtype))
        hp_l.append(hp)
        ks_l.append(kss)
        vs_l.append(vss)
        ss_l.append(Ss.astype(state_rwkv.dtype))
        hs_l.append(hs.astype(state_shift.dtype))
    y_prompt = rms_norm(xp, final_norm_w, RMS_EPS).astype(x_prompt.dtype)
    y_sample = rms_norm(xs, final_norm_w, RMS_EPS).astype(x_sample.dtype)
    k_prompt = jnp.stack(kp_l, 0)
    v_prompt = jnp.stack(vp_l, 0)
    rwkv_prompt = jnp.stack(sp_l, 0)
    shift_prompt = jnp.stack(hp_l, 0)
    k_sample = jnp.stack(ks_l, 0)
    v_sample = jnp.stack(vs_l, 0)
    rwkv_sample = jnp.stack(ss_l, 0)
    shift_sample = jnp.stack(hs_l, 0)
    return (y_prompt, y_sample, k_prompt, v_prompt, rwkv_prompt, shift_prompt,
            k_sample, v_sample, rwkv_sample, shift_sample)
```

```python
import functools
import math

import jax
import jax.numpy as jnp
from jax import lax
from jax.experimental import pallas as pl
from jax.experimental.pallas import tpu as pltpu

F32 = jnp.float32
BF16 = jnp.bfloat16

RMS_EPS = 1e-6
SUBLN_EPS = 1e-5
GN_EPS = 64e-5

LANES = 128
HEAD_DIM = 64
PAIR = 2 * HEAD_DIM
VMEM_LIMIT = 48 * 1024 * 1024

_NT = (((1,), (1,)), ((), ()))
_TN = (((0,), (0,)), ((), ()))


def _dot(a, b):
    return jnp.dot(a, b, preferred_element_type=F32)


def _dot_nt(a, b):
    return lax.dot_general(a, b, _NT, preferred_element_type=F32)


def _dot_tn(a, b):
    return lax.dot_general(a, b, _TN, preferred_element_type=F32)


def _sigmoid(x):
    return 1.0 / (1.0 + jnp.exp(-x))


def _silu(x):
    return x * _sigmoid(x)


def _params(sem):
    return pltpu.CompilerParams(dimension_semantics=sem, vmem_limit_bytes=VMEM_LIMIT)


def _proj_kernel(x_ref, nw_ref, w_ref, cols_ref, rg_ref, q_ref, k_ref, v_ref, kb_ref, vb_ref, dg_ref,
                 *, bounds):
    x = x_ref[...]
    h = x * lax.rsqrt(jnp.mean(x * x, axis=-1, keepdims=True) + RMS_EPS) * nw_ref[...]
    hb = h.astype(BF16)
    c0, c1, c2, c3, c4, c5 = bounds
    cols_ref[...] = _dot(hb, w_ref[:, 0:c0])
    rg_ref[...] = _dot(hb, w_ref[:, c0:c1])
    q_ref[...] = _dot(hb, w_ref[:, c1:c2]).astype(BF16)
    k = _dot(hb, w_ref[:, c2:c3])
    k_ref[...] = k
    kb_ref[...] = k.astype(BF16)
    v = _dot(hb, w_ref[:, c3:c4])
    v_ref[...] = v
    vb_ref[...] = v.astype(BF16)
    dg_ref[...] = _dot(hb, w_ref[:, c4:c5])


def _proj(x2d, norm_w, w_bf16, bounds, tm):
    m, d = x2d.shape
    c0, c1, c2, c3, c4, c5 = bounds
    widths = (c0, c1 - c0, c2 - c1, c3 - c2, c4 - c3, c3 - c2, c4 - c3, c5 - c4)
    dtypes = (F32, F32, BF16, F32, F32, BF16, BF16, F32)
    row = lambda i: (i, 0)
    const = lambda i: (0, 0)
    return pl.pallas_call(
        functools.partial(_proj_kernel, bounds=bounds),
        grid=(m // tm,),
        in_specs=[pl.BlockSpec((tm, d), row), pl.BlockSpec((1, d), const), pl.BlockSpec(w_bf16.shape, const)],
        out_specs=[pl.BlockSpec((tm, w), row) for w in widths],
        out_shape=[jax.ShapeDtypeStruct((m, w), dt) for w, dt in zip(widths, dtypes)],
        compiler_params=_params(("arbitrary",)),
        name="proj",
    )(x2d, norm_w.reshape(1, d), w_bf16)


def _head_sum(x):
    rows, width = x.shape
    lo = lax.broadcasted_iota(jnp.int32, (rows, PAIR), 1) < HEAD_DIM
    out = []
    for p in range(width // PAIR):
        xp = x[:, p * PAIR:(p + 1) * PAIR]
        s_lo = jnp.sum(jnp.where(lo, xp, 0.0), axis=-1, keepdims=True)
        s_hi = jnp.sum(jnp.where(lo, 0.0, xp), axis=-1, keepdims=True)
        out.append(jnp.where(lo, s_lo, s_hi))
    return jnp.concatenate(out, axis=-1)


def _split3(x):
    h0 = x.astype(BF16)
    r1 = x - h0.astype(F32)
    h1 = r1.astype(BF16)
    h2 = (r1 - h1.astype(F32)).astype(BF16)
    return h0, h1, h2


def _rwkv_kernel(cols_ref, gate_ref, prev0_ref, s0_ref, mu_ref, w0_ref, lora_ref, a0_ref, keyk_ref, keya_ref,
                 bonus_ref, lnw_ref, lnb_ref, out_ref, sout_ref, s_scr, prev_scr,
                 *, chunk, nsub, width, t_valid):
    tb = pl.program_id(1)
    nblk = pl.num_programs(1)
    rows = chunk * nsub
    npair = width // PAIR
    c2 = 2 * chunk

    @pl.when(tb == 0)
    def _():
        prev_scr[...] = prev0_ref[0]
        zero = jnp.zeros((HEAD_DIM, HEAD_DIM), F32)
        for p in range(npair):
            top = jnp.concatenate([s0_ref[0, 2 * p], zero], axis=1)
            bot = jnp.concatenate([zero, s0_ref[0, 2 * p + 1]], axis=1)
            s_scr[p] = jnp.concatenate([top, bot], axis=0)

    c = cols_ref[...]
    row = lax.broadcasted_iota(jnp.int32, c.shape, 0)
    shifted = jnp.where(row == 0, prev_scr[...], pltpu.roll(c, 1, 0))
    prev_scr[...] = c[rows - 1:rows, :]
    xs = c + (shifted - c) * mu_ref[...]
    r = xs[:, 0:width]
    k = xs[:, width:2 * width]
    v = xs[:, 2 * width:3 * width]
    xwa = xs[:, 3 * width:3 * width + PAIR]
    lo_rank = lax.broadcasted_iota(jnp.int32, xwa.shape, 1) < HEAD_DIM
    xwa = jnp.where(lo_rank, jnp.tanh(xwa), xwa)
    lora = _dot(xwa.astype(BF16), lora_ref[...])
    z = -(w0_ref[...] + lora[:, 0:width])
    softplus = jnp.maximum(z, 0.0) + jnp.log(1.0 + jnp.exp(-jnp.abs(z)))
    lw = -jnp.exp(-softplus - 0.5)
    a = _sigmoid(a0_ref[...] + lora[:, width:2 * width])
    kk = k * keyk_ref[...]
    kk = kk / jnp.maximum(jnp.sqrt(_head_sum(kk * kk)), 1e-12)
    k2 = k * (1.0 + (a - 1.0) * keya_ref[...])
    av = -kk
    bv = kk * a
    if t_valid < rows:
        live = lax.broadcasted_iota(jnp.int32, lw.shape, 0) < t_valid
        lw = jnp.where(live, lw, 0.0)
        av = jnp.where(live, av, 0.0)
        bv = jnp.where(live, bv, 0.0)
        k2s = jnp.where(live, k2, 0.0)
        vs = jnp.where(live, v, 0.0)
    else:
        k2s, vs = k2, v

    ri = lax.broadcasted_iota(jnp.int32, (c2, c2), 0)
    ci = lax.broadcasted_iota(jnp.int32, (c2, c2), 1)
    same = (ri >= chunk) == (ci >= chunk)
    strict = same & (ci < ri)
    incl = same & (ci <= ri)
    eye = (ri == ci).astype(F32)
    tri = (lax.broadcasted_iota(jnp.int32, (chunk, chunk), 1)
           <= lax.broadcasted_iota(jnp.int32, (chunk, chunk), 0)).astype(BF16)
    lo = lax.broadcasted_iota(jnp.int32, (chunk, PAIR), 1) < HEAD_DIM
    nsteps = max(1, (chunk - 1).bit_length())

    def stack(xp):
        return jnp.concatenate([jnp.where(lo, xp, 0.0), jnp.where(lo, 0.0, xp)], axis=0)

    y_chunks = []
    for sub in range(nsub):
        sl = slice(sub * chunk, (sub + 1) * chunk)
        lwc = lw[sl]
        g0, g1, g2 = _split3(lwc)
        g = _dot(tri, g0) + _dot(tri, g1) + _dot(tri, g2)
        gp = g - lwc
        gc = g[chunk - 1:chunk, :]
        eg = jnp.exp(g)
        eng = jnp.exp(-g)
        egc = jnp.exp(gc - g)
        at = av[sl] * jnp.exp(gp)
        rt = r[sl] * eg
        bt = bv[sl] * eng
        kt = k2s[sl] * eng
        bh = bv[sl] * egc
        kh = k2s[sl] * egc
        egc_last = jnp.exp(gc)
        vc = vs[sl]
        y_pairs = []
        for p in range(npair):
            ps = slice(p * PAIR, (p + 1) * PAIR)
            la, lr, lb, lk = stack(at[:, ps]), stack(rt[:, ps]), stack(bt[:, ps]), stack(kt[:, ps])
            lbh, lkh, lv = stack(bh[:, ps]), stack(kh[:, ps]), stack(vc[:, ps])
            lab = la.astype(BF16)
            lrb = lr.astype(BF16)
            lvb = lv.astype(BF16)
            sc = _dot_nt(jnp.concatenate([lab, lrb], axis=0),
                         jnp.concatenate([lb.astype(BF16), lk.astype(BF16)], axis=0))
            dab = jnp.where(strict, sc[0:c2, 0:c2], 0.0)
            dak = jnp.where(strict, sc[0:c2, c2:2 * c2], 0.0)
            drb = jnp.where(incl, sc[c2:2 * c2, 0:c2], 0.0)
            drk = jnp.where(incl, sc[c2:2 * c2, c2:2 * c2], 0.0)
            pw = dab
            inv = eye + pw
            for step in range(nsteps):
                pb = pw.astype(BF16)
                if step == 0:
                    pw = _dot(pb, pb)
                elif step < nsteps - 1:
                    px = _dot(pb, jnp.concatenate([pb, inv.astype(BF16)], axis=1))
                    pw = px[:, 0:c2]
                    inv = inv + px[:, c2:2 * c2]
                else:
                    inv = inv + _dot(pb, inv.astype(BF16))
            dv = _dot(jnp.concatenate([dak, drk], axis=0).astype(BF16), lvb)
            tw = _dot(inv.astype(BF16), jnp.concatenate([lab, dv[0:c2].astype(BF16)], axis=1))
            lw_, lu0 = tw[:, 0:PAIR], tw[:, PAIR:2 * PAIR]
            sp = s_scr[p]
            ws = _dot_nt(jnp.concatenate([lw_.astype(BF16), lrb], axis=0), sp.astype(BF16))
            lu = ws[0:c2] + lu0
            lub = lu.astype(BF16)
            lo_ = ws[c2:2 * c2] + _dot(drb.astype(BF16), lub) + dv[c2:2 * c2]
            y_pairs.append(lo_[0:chunk] + lo_[chunk:c2])
            s_scr[p] = sp * egc_last[:, ps] + _dot_tn(
                jnp.concatenate([lub, lvb], axis=0),
                jnp.concatenate([lbh.astype(BF16), lkh.astype(BF16)], axis=0))
        y_chunks.append(jnp.concatenate(y_pairs, axis=-1))
    y = jnp.concatenate(y_chunks, axis=0) if nsub > 1 else y_chunks[0]

    inv_n = 1.0 / HEAD_DIM
    mean = _head_sum(y) * inv_n
    d = y - mean
    var = _head_sum(d * d) * inv_n
    yn = d * lax.rsqrt(var + GN_EPS) * lnw_ref[...] + lnb_ref[...]
    bonus = _head_sum(r * k2 * bonus_ref[...]) * v
    out_ref[...] = (yn + bonus) * _silu(gate_ref[...])

    @pl.when(tb == nblk - 1)
    def _():
        for p in range(npair):
            sp = s_scr[p]
            sout_ref[0, 2 * p] = sp[0:HEAD_DIM, 0:HEAD_DIM]
            sout_ref[0, 2 * p + 1] = sp[HEAD_DIM:PAIR, HEAD_DIM:PAIR]


def _rwkv(cols, gate, prev0, s0, lp, *, batch, chunk, nsub, t_valid):
    m, ncols = cols.shape
    width = gate.shape[1]
    rows = chunk * nsub
    nblk = m // batch // rows
    heads = width // HEAD_DIM
    blk = lambda b, t: (b * nblk + t, 0)
    const = lambda b, t: (0, 0)
    vec = lambda n: pl.BlockSpec((1, n), const)
    return pl.pallas_call(
        functools.partial(_rwkv_kernel, chunk=chunk, nsub=nsub, width=width, t_valid=t_valid),
        grid=(batch, nblk),
        in_specs=[pl.BlockSpec((rows, ncols), blk), pl.BlockSpec((rows, width), blk),
                  pl.BlockSpec((1, 1, ncols), lambda b, t: (b, 0, 0)),
                  pl.BlockSpec((1, heads, HEAD_DIM, HEAD_DIM), lambda b, t: (b, 0, 0, 0)),
                  vec(ncols), vec(width), pl.BlockSpec((PAIR, 2 * width), const), vec(width), vec(width),
                  vec(width), vec(width), vec(width), vec(width)],
        out_specs=[pl.BlockSpec((rows, width), blk),
                   pl.BlockSpec((1, heads, HEAD_DIM, HEAD_DIM), lambda b, t: (b, 0, 0, 0))],
        out_shape=[jax.ShapeDtypeStruct((m, width), F32),
                   jax.ShapeDtypeStruct((batch, heads, HEAD_DIM, HEAD_DIM), F32)],
        scratch_shapes=[pltpu.VMEM((width // PAIR, PAIR, PAIR), F32), pltpu.VMEM((1, ncols), F32)],
        compiler_params=_params(("arbitrary", "arbitrary")),
        name="rwkv",
    )(cols, gate, prev0.reshape(batch, 1, ncols), s0, lp["mu"], lp["w0"], lp["lora"], lp["a0"], lp["key_k"],
      lp["key_a"], lp["bonus"], lp["lnx_w"], lp["lnx_b"])


def _lambda(lam_ref, lam_init):
    lam = lam_ref[...]
    e1 = jnp.exp(jnp.sum(lam[0:1] * lam[1:2], axis=-1, keepdims=True))
    e2 = jnp.exp(jnp.sum(lam[2:3] * lam[3:4], axis=-1, keepdims=True))
    return e1 - e2 + lam_init


def _subln_gate(o, subw, gate, lam_init):
    o = o * lax.rsqrt(jnp.mean(o * o, axis=-1, keepdims=True) + SUBLN_EPS) * subw
    return o * (1.0 - lam_init) * _silu(gate)


def _attn_prompt_kernel(q_ref, k_ref, v_ref, dg_ref, subw_ref, lam_ref, out_ref, *, tq, lam_init):
    i = pl.program_id(2)
    lo = lax.broadcasted_iota(jnp.int32, (tq, PAIR), 1) < HEAD_DIM
    q = q_ref[...] * BF16(1.0 / math.sqrt(HEAD_DIM))
    zero = jnp.zeros_like(q)
    qs = jnp.concatenate([jnp.where(lo, q, zero), jnp.where(lo, zero, q)], axis=0)

    def block(j, carry, masked):
        m, l, acc = carry
        kj = k_ref[pl.ds(pl.multiple_of(j * tq, tq), tq), :]
        vj = v_ref[pl.ds(pl.multiple_of(j * tq, tq), tq), :]
        s = _dot_nt(qs, kj)
        if masked:
            qi = lax.broadcasted_iota(jnp.int32, (2 * tq, tq), 0)
            qi = jnp.where(qi >= tq, qi - tq, qi)
            ki = lax.broadcasted_iota(jnp.int32, (2 * tq, tq), 1)
            s = jnp.where(ki <= qi, s, -jnp.inf)
        m_new = jnp.maximum(m, jnp.max(s, axis=-1, keepdims=True))
        alpha = jnp.exp(m - m_new)
        p = jnp.exp(s - m_new)
        l = alpha * l + jnp.sum(p, axis=-1, keepdims=True)
        acc = alpha * acc + _dot(p.astype(BF16), vj)
        return m_new, l, acc

    init = (jnp.full((2 * tq, 1), -jnp.inf, F32), jnp.zeros((2 * tq, 1), F32), jnp.zeros((2 * tq, PAIR), F32))
    carry = lax.fori_loop(0, i, lambda j, c: block(j, c, False), init)
    m, l, acc = block(i, carry, True)
    o = acc / l
    o = o[0:tq] - _lambda(lam_ref, lam_init) * o[tq:2 * tq]
    out_ref[...] = _subln_gate(o, subw_ref[...], dg_ref[...], lam_init)


def _attn_prompt(q, kb, vb, dg, subw, lam_vecs, *, batch, seq, tq, lam_init):
    m, width = q.shape
    heads = width // PAIR
    nq = seq // tq
    qblk = lambda b, h, i: (b * nq + i, h)
    kvblk = lambda b, h, i: (b, h)
    const = lambda b, h, i: (0, 0)
    return pl.pallas_call(
        functools.partial(_attn_prompt_kernel, tq=tq, lam_init=lam_init),
        grid=(batch, heads, nq),
        in_specs=[pl.BlockSpec((tq, PAIR), qblk), pl.BlockSpec((seq, PAIR), kvblk), pl.BlockSpec((seq, PAIR), kvblk),
                  pl.BlockSpec((tq, PAIR), qblk), pl.BlockSpec((1, PAIR), const), pl.BlockSpec((4, HEAD_DIM), const)],
        out_specs=pl.BlockSpec((tq, PAIR), qblk),
        out_shape=jax.ShapeDtypeStruct((m, width), F32),
        compiler_params=_params(("arbitrary", "arbitrary", "arbitrary")),
        name="attn_prompt",
    )(q, kb, vb, dg, subw, lam_vecs)


def _attn_sample_kernel(pt_ref, q_ref, kn_ref, vn_ref, dg_ref, subw_ref, lam_ref, *rest,
                        npages, page, tpad, heads, lam_init):
    del pt_ref
    kp = rest[0:npages]
    vp = rest[npages:2 * npages]
    out_ref = rest[2 * npages]
    width = heads * PAIR
    nrow = 2 * heads * tpad
    q = q_ref[0] * (1.0 / math.sqrt(HEAD_DIM))
    qrep = jnp.concatenate([q] * (2 * heads), axis=0)
    ri = lax.broadcasted_iota(jnp.int32, (nrow, width), 0) // tpad
    ci = lax.broadcasted_iota(jnp.int32, (nrow, width), 1) // HEAD_DIM
    sel = (ri % heads) * 2 + ri // heads == ci
    qbd = jnp.where(sel, qrep, 0.0).astype(BF16)

    def two(refs, j):
        return jnp.concatenate([refs[j][...].astype(BF16), refs[j + 1][...].astype(BF16)], axis=0)

    scores = [_dot_nt(qbd, two(kp, j)) for j in range(0, npages, 2)]
    knew = jnp.concatenate([kn_ref[0], jnp.zeros((LANES - tpad, width), F32)], axis=0).astype(BF16)
    s_new = _dot_nt(qbd, knew)
    tq_i = lax.broadcasted_iota(jnp.int32, (nrow, LANES), 0) % tpad
    tk_i = lax.broadcasted_iota(jnp.int32, (nrow, LANES), 1)
    s_new = jnp.where(tk_i <= tq_i, s_new, -jnp.inf)
    m = jnp.max(s_new, axis=-1, keepdims=True)
    for s in scores:
        m = jnp.maximum(m, jnp.max(s, axis=-1, keepdims=True))
    p_new = jnp.exp(s_new - m)
    l = jnp.sum(p_new, axis=-1, keepdims=True)
    probs = []
    for s in scores:
        p = jnp.exp(s - m)
        l = l + jnp.sum(p, axis=-1, keepdims=True)
        probs.append(p)
    inv_l = 1.0 / l
    lam = _lambda(lam_ref, lam_init)
    half = nrow // 2
    w1 = inv_l[0:half]
    w2 = lam * inv_l[half:nrow]

    def mix(p):
        return (p[0:half] * w1 - p[half:nrow] * w2).astype(BF16)

    vnew = jnp.concatenate([vn_ref[0], jnp.zeros((LANES - tpad, width), F32)], axis=0).astype(BF16)
    o = _dot(mix(p_new), vnew)
    for idx, j in enumerate(range(0, npages, 2)):
        o = o + _dot(mix(probs[idx]), two(vp, j))
    dg = dg_ref[0]
    outs = []
    for h in range(heads):
        oh = o[h * tpad:(h + 1) * tpad, h * PAIR:(h + 1) * PAIR]
        outs.append(_subln_gate(oh, subw_ref[...], dg[:, h * PAIR:(h + 1) * PAIR], lam_init))
    out_ref[0] = jnp.concatenate(outs, axis=-1)


def _attn_sample(q, kn, vn, dg, subw, lam_vecs, cache_k, cache_v, page_table, *, layer, lam_init):
    batch, tpad, width = q.shape
    npages = page_table.shape[1]
    n_phys, page = cache_k.shape[1], cache_k.shape[2]
    heads = width // PAIR
    ck = cache_k.reshape(cache_k.shape[0] * n_phys, page, width)
    cv = cache_v.reshape(cache_v.shape[0] * n_phys, page, width)
    pt = page_table.reshape(-1).astype(jnp.int32) + layer * n_phys
    tok = pl.BlockSpec((1, tpad, width), lambda b, pt: (b, 0, 0))
    const = lambda b, pt: (0, 0)

    def page_spec(j):
        return pl.BlockSpec((None, page, width), lambda b, pt: (pt[b * npages + j], 0, 0))

    grid_spec = pltpu.PrefetchScalarGridSpec(
        num_scalar_prefetch=1,
        grid=(batch,),
        in_specs=[tok, tok, tok, tok, pl.BlockSpec((1, PAIR), const), pl.BlockSpec((4, HEAD_DIM), const)]
        + [page_spec(j) for j in range(npages)] + [page_spec(j) for j in range(npages)],
        out_specs=tok,
    )
    return pl.pallas_call(
        functools.partial(_attn_sample_kernel, npages=npages, page=page, tpad=tpad, heads=heads, lam_init=lam_init),
        grid_spec=grid_spec,
        out_shape=jax.ShapeDtypeStruct((batch, tpad, width), F32),
        compiler_params=_params(("arbitrary",)),
        name="attn_sample",
    )(pt, q, kn, vn, dg, subw, lam_vecs, *([ck] * npages), *([cv] * npages))


def _outproj_kernel(x_ref, r_ref, d_ref, w_ref, fw_ref, o_ref, *, final):
    half = r_ref.shape[1]
    y = x_ref[...] + _dot(r_ref[...].astype(BF16), w_ref[0:half, :]) + _dot(d_ref[...].astype(BF16), w_ref[half:, :])
    if final:
        y = y * lax.rsqrt(jnp.mean(y * y, axis=-1, keepdims=True) + RMS_EPS) * fw_ref[...]
    o_ref[...] = y


def _outproj(x2d, rw, df, w_bf16, final_w, *, final, tm):
    m, d = x2d.shape
    half = rw.shape[1]
    row = lambda i: (i, 0)
    const = lambda i: (0, 0)
    return pl.pallas_call(
        functools.partial(_outproj_kernel, final=final),
        grid=(m // tm,),
        in_specs=[pl.BlockSpec((tm, d), row), pl.BlockSpec((tm, half), row), pl.BlockSpec((tm, half), row),
                  pl.BlockSpec(w_bf16.shape, const), pl.BlockSpec((1, d), const)],
        out_specs=pl.BlockSpec((tm, d), row),
        out_shape=jax.ShapeDtypeStruct((m, d), F32),
        compiler_params=_params(("arbitrary",)),
        name="outproj",
    )(x2d, rw, df, w_bf16, final_w.reshape(1, d))


def kernel(x_prompt, x_sample, cache_k, cache_v, page_table, state_rwkv, state_shift, norm_w, w_in, shift_mu, decay_w0, decay_lora_b, iclr_a0, iclr_lora_b, key_k, key_a, bonus_r_k, lnx_w, lnx_b, lam_q1, lam_k1, lam_q2, lam_k2, subln_w, w_out, final_norm_w):
    depth = w_in.shape[0]
    bp, seq, d = x_prompt.shape
    bs, tdec, _ = x_sample.shape
    width = decay_w0.shape[-1]
    ncols = shift_mu.shape[-1]
    rank_w, rank_a = decay_lora_b.shape[1], iclr_lora_b.shape[1]
    heads_r = width // HEAD_DIM
    heads_d = cache_k.shape[3]
    dwidth = heads_d * cache_v.shape[4]
    dqk = heads_d * cache_k.shape[4]
    assert rank_w == HEAD_DIM and rank_a == HEAD_DIM and ncols == 3 * width + PAIR
    assert cache_k.shape[4] == PAIR and cache_v.shape[4] == PAIR
    c0 = ncols
    c1 = c0 + width
    c2 = c1 + dqk
    c3 = c2 + dqk
    c4 = c3 + dwidth
    c5 = c4 + dwidth
    bounds = (c0, c1, c2, c3, c4, c5)
    assert w_in.shape[2] == c5

    tpad = 8
    assert tdec <= tpad
    xp = x_prompt.reshape(bp * seq, d)
    xs = jnp.pad(x_sample, ((0, 0), (0, tpad - tdec), (0, 0))).reshape(bs * tpad, d)

    outs = {n: [] for n in ("kp", "vp", "sp", "hp", "ks", "vs", "ss", "hs")}
    for l in range(depth):
        lam_init = 0.8 - 0.6 * math.exp(-0.3 * l)
        w_l = w_in[l].astype(BF16)
        wo_l = w_out[l].astype(BF16)
        zeros = jnp.zeros((rank_w, width), F32)
        lora = jnp.concatenate([jnp.concatenate([decay_lora_b[l], zeros], axis=1),
                                jnp.concatenate([zeros, iclr_lora_b[l]], axis=1)], axis=0).astype(BF16)
        lp = dict(mu=shift_mu[l].reshape(1, ncols), w0=decay_w0[l].reshape(1, width), lora=lora,
                  a0=iclr_a0[l].reshape(1, width), key_k=key_k[l].reshape(1, width), key_a=key_a[l].reshape(1, width),
                  bonus=bonus_r_k[l].reshape(1, width), lnx_w=lnx_w[l].reshape(1, width),
                  lnx_b=lnx_b[l].reshape(1, width))
        lam_vecs = jnp.stack([lam_q1[l], lam_k1[l], lam_q2[l], lam_k2[l]], axis=0)
        subw = subln_w[l].reshape(1, PAIR)
        last = l == depth - 1

        cols, rg, q, k, v, kb, vb, dg = _proj(xp, norm_w[l], w_l, bounds, tm=256)
        rw, s_new = _rwkv(cols, rg, jnp.zeros((bp, ncols), F32), jnp.zeros((bp, heads_r, HEAD_DIM, HEAD_DIM), F32),
                          lp, batch=bp, chunk=64, nsub=2, t_valid=128)
        df = _attn_prompt(q, kb, vb, dg, subw, lam_vecs, batch=bp, seq=seq, tq=256, lam_init=lam_init)
        xp = _outproj(xp, rw, df, wo_l, final_norm_w, final=last, tm=256)
        outs["kp"].append(k.reshape(bp, seq, heads_d, PAIR))
        outs["vp"].append(v.reshape(bp, seq, heads_d, PAIR))
        outs["sp"].append(s_new)
        outs["hp"].append(cols.reshape(bp, seq, ncols)[:, -1, :])

        cols, rg, q, k, v, kb, vb, dg = _proj(xs, norm_w[l], w_l, bounds, tm=256)
        rw, s_new = _rwkv(cols, rg, state_shift[l], state_rwkv[l], lp, batch=bs, chunk=tpad, nsub=1, t_valid=tdec)
        r3 = lambda t: t.reshape(bs, tpad, -1)
        df = _attn_sample(r3(q), r3(k), r3(v), r3(dg), subw, lam_vecs, cache_k, cache_v, page_table,
                          layer=l, lam_init=lam_init)
        xs = _outproj(xs, rw, df.reshape(bs * tpad, dwidth), wo_l, final_norm_w, final=last, tm=256)
        outs["ks"].append(r3(k)[:, :tdec].reshape(bs, tdec, heads_d, PAIR))
        outs["vs"].append(r3(v)[:, :tdec].reshape(bs, tdec, heads_d, PAIR))
        outs["ss"].append(s_new)
        outs["hs"].append(r3(cols)[:, tdec - 1, :])

    y_prompt = xp.reshape(bp, seq, d)
    y_sample = xs.reshape(bs, tpad, d)[:, :tdec]
    st = lambda n: jnp.stack(outs[n], axis=0)
    return (y_prompt, y_sample, st("kp"), st("vp"), st("sp"), st("hp"), st("ks"), st("vs"), st("ss"), st("hs"))
```

```python
import functools
import math

import jax
import jax.numpy as jnp
from jax import lax
from jax.experimental import pallas as pl
from jax.experimental.pallas import tpu as pltpu

F32 = jnp.float32
BF16 = jnp.bfloat16

RMS_EPS = 1e-6
SUBLN_EPS = 1e-5
GN_EPS = 64e-5

LANES = 128
SUBLANES = 8
HEAD_DIM = 64
PAIR = 2 * HEAD_DIM
VMEM_LIMIT = 48 * 1024 * 1024

_NT = (((1,), (1,)), ((), ()))
_TN = (((0,), (0,)), ((), ()))


def _dot(a, b):
    return jnp.dot(a, b, preferred_element_type=F32)


def _dot_nt(a, b):
    return lax.dot_general(a, b, _NT, preferred_element_type=F32)


def _dot_tn(a, b):
    return lax.dot_general(a, b, _TN, preferred_element_type=F32)


def _sigmoid(x):
    return 1.0 / (1.0 + jnp.exp(-x))


def _silu(x):
    return x * _sigmoid(x)


def _cat(xs, axis):
    return xs[0] if len(xs) == 1 else jnp.concatenate(xs, axis=axis)


def _params(sem):
    return pltpu.CompilerParams(dimension_semantics=sem, vmem_limit_bytes=VMEM_LIMIT)


def _proj_kernel(x_ref, nw_ref, w_ref, *out_refs, bounds, prompt):
    x = x_ref[...]
    h = x * lax.rsqrt(jnp.mean(x * x, axis=-1, keepdims=True) + RMS_EPS) * nw_ref[...]
    hb = h.astype(BF16)
    c0, c1, c2, c3, c4, c5 = bounds
    mm = lambda lo, hi: _dot(hb, w_ref[:, lo:hi])
    if prompt:
        cols_ref, rg_ref, q_ref, k_ref, v_ref, kb_ref, vt_ref, dg_ref = out_refs
        q_ref[...] = mm(c1, c2).astype(BF16)
    else:
        cols_ref, rg_ref, q_ref, k_ref, v_ref, dg_ref = out_refs
        q_ref[...] = mm(c1, c2)
    cols_ref[...] = mm(0, c0)
    rg_ref[...] = mm(c0, c1)
    k = mm(c2, c3)
    k_ref[...] = k
    v = mm(c3, c4)
    v_ref[...] = v
    dg_ref[...] = mm(c4, c5)
    if prompt:
        kb_ref[...] = k.astype(BF16)
        vt_ref[0] = v.T.astype(BF16)


def _proj(x2d, norm_w, w_bf16, bounds, *, tm, prompt, seq=None):
    m, d = x2d.shape
    c0, c1, c2, c3, c4, c5 = bounds
    row = lambda i: (i, 0)
    const = lambda i: (0, 0)
    wide = lambda w, dt: (pl.BlockSpec((tm, w), row), jax.ShapeDtypeStruct((m, w), dt))
    outs = [wide(c0, F32), wide(c1 - c0, F32), wide(c2 - c1, BF16 if prompt else F32),
            wide(c3 - c2, F32), wide(c4 - c3, F32)]
    if prompt:
        nt = seq // tm
        outs.append(wide(c3 - c2, BF16))
        outs.append((pl.BlockSpec((1, c4 - c3, tm), lambda i: (i // nt, 0, i % nt)),
                     jax.ShapeDtypeStruct((m // seq, c4 - c3, seq), BF16)))
    outs.append(wide(c5 - c4, F32))
    return pl.pallas_call(
        functools.partial(_proj_kernel, bounds=bounds, prompt=prompt),
        grid=(m // tm,),
        in_specs=[pl.BlockSpec((tm, d), row), pl.BlockSpec((1, d), const), pl.BlockSpec(w_bf16.shape, const)],
        out_specs=[o[0] for o in outs],
        out_shape=[o[1] for o in outs],
        compiler_params=_params(("arbitrary",)),
        name="proj_prompt" if prompt else "proj_sample",
    )(x2d, norm_w.reshape(1, d), w_bf16)


def _head_sum(x):
    rows, width = x.shape
    lo = lax.broadcasted_iota(jnp.int32, (rows, PAIR), 1) < HEAD_DIM
    out = []
    for p in range(width // PAIR):
        xp = x[:, p * PAIR:(p + 1) * PAIR]
        s_lo = jnp.sum(jnp.where(lo, xp, 0.0), axis=-1, keepdims=True)
        s_hi = jnp.sum(jnp.where(lo, 0.0, xp), axis=-1, keepdims=True)
        out.append(jnp.where(lo, s_lo, s_hi))
    return jnp.concatenate(out, axis=-1)


def _split3(x):
    h0 = x.astype(BF16)
    r1 = x - h0.astype(F32)
    h1 = r1.astype(BF16)
    h2 = (r1 - h1.astype(F32)).astype(BF16)
    return h0, h1, h2


def _blockdiag(s_even, s_odd):
    zero = jnp.zeros((HEAD_DIM, HEAD_DIM), F32)
    return jnp.concatenate([jnp.concatenate([s_even, zero], axis=1), jnp.concatenate([zero, s_odd], axis=1)], axis=0)


def _rwkv_kernel(cols_ref, gate_ref, prev_ref, s0_ref, mu_ref, w0_ref, lora_ref, a0_ref, keyk_ref, keya_ref,
                 bonus_ref, lnw_ref, lnb_ref, out_ref, sout_ref, s_scr, prev_scr,
                 *, rows, tlen, nsub, width, t_valid, carry):
    tb = pl.program_id(1)
    nblk = pl.num_programs(1)
    npair = width // PAIR
    nseq = rows // tlen
    r2 = 2 * rows
    blk_rows = rows * nsub

    c = cols_ref[...]
    row_t = lax.broadcasted_iota(jnp.int32, c.shape, 0) % tlen if not carry else None
    if carry:
        @pl.when(tb == 0)
        def _():
            prev_scr[...] = prev_ref[0]
            for p in range(npair):
                s_scr[p] = _blockdiag(s0_ref[0, 2 * p], s0_ref[0, 2 * p + 1])

        row = lax.broadcasted_iota(jnp.int32, c.shape, 0)
        shifted = jnp.where(row == 0, prev_scr[...], pltpu.roll(c, 1, 0))
        prev_scr[...] = c[blk_rows - 1:blk_rows, :]
        state = {(0, p): s_scr[p] for p in range(npair)}
    else:
        shifted = jnp.where(row_t == 0, prev_ref[...], pltpu.roll(c, 1, 0))
        state = {(q, p): _blockdiag(s0_ref[q, 2 * p], s0_ref[q, 2 * p + 1])
                 for q in range(nsub * nseq) for p in range(npair)}

    xs = c + (shifted - c) * mu_ref[...]
    r = xs[:, 0:width]
    k = xs[:, width:2 * width]
    v = xs[:, 2 * width:3 * width]
    xwa = xs[:, 3 * width:3 * width + PAIR]
    lo_rank = lax.broadcasted_iota(jnp.int32, xwa.shape, 1) < HEAD_DIM
    xwa = jnp.where(lo_rank, jnp.tanh(xwa), xwa)
    lora = _dot(xwa.astype(BF16), lora_ref[...])
    z = -(w0_ref[...] + lora[:, 0:width])
    softplus = jnp.maximum(z, 0.0) + jnp.log(1.0 + jnp.exp(-jnp.abs(z)))
    lw = -jnp.exp(-softplus - 0.5)
    a = _sigmoid(a0_ref[...] + lora[:, width:2 * width])
    kk = k * keyk_ref[...]
    kk = kk / jnp.maximum(jnp.sqrt(_head_sum(kk * kk)), 1e-12)
    k2 = k * (1.0 + (a - 1.0) * keya_ref[...])
    av = -kk
    bv = kk * a
    if t_valid < tlen:
        live = lax.broadcasted_iota(jnp.int32, lw.shape, 0) % tlen < t_valid
        lw = jnp.where(live, lw, 0.0)
        av = jnp.where(live, av, 0.0)
        bv = jnp.where(live, bv, 0.0)
        k2s = jnp.where(live, k2, 0.0)
        vs = jnp.where(live, v, 0.0)
    else:
        k2s, vs = k2, v

    ri = lax.broadcasted_iota(jnp.int32, (r2, r2), 0)
    ci = lax.broadcasted_iota(jnp.int32, (r2, r2), 1)
    same = (ri // tlen) == (ci // tlen)
    strict = same & (ci < ri)
    incl = same & (ci <= ri)
    eye = (ri == ci).astype(F32)
    cum_lower = incl[0:rows, 0:rows]
    cum = jnp.concatenate([cum_lower, same[0:rows, 0:rows]], axis=0).astype(F32).astype(BF16)
    lo = lax.broadcasted_iota(jnp.int32, (rows, PAIR), 1) < HEAD_DIM
    nsteps = max(1, (tlen - 1).bit_length())

    def stack(xp):
        return jnp.concatenate([jnp.where(lo, xp, 0.0), jnp.where(lo, 0.0, xp)], axis=0)

    def seq_rows(x, q):
        if nseq == 1:
            return x
        return jnp.concatenate([x[q * tlen:(q + 1) * tlen], x[rows + q * tlen:rows + (q + 1) * tlen]], axis=0)

    def unseq(parts):
        if nseq == 1:
            return parts[0]
        return jnp.concatenate([x[0:tlen] for x in parts] + [x[tlen:2 * tlen] for x in parts], axis=0)

    probs = []
    for sub in range(nsub):
        sl = slice(sub * rows, (sub + 1) * rows)
        lwc = lw[sl]
        g0, g1, g2 = _split3(lwc)
        gg = _dot(cum, g0) + _dot(cum, g1) + _dot(cum, g2)
        g = gg[0:rows]
        g_end = gg[rows:r2]
        eg = jnp.exp(g)
        eng = jnp.exp(-g)
        egc = jnp.exp(g_end - g)
        at = av[sl] * jnp.exp(g - lwc)
        rt = r[sl] * eg
        bt = bv[sl] * eng
        kt = k2s[sl] * eng
        bh = bv[sl] * egc
        kh = k2s[sl] * egc
        e_end = jnp.exp(g_end)
        vc = vs[sl]
        for p in range(npair):
            ps = slice(p * PAIR, (p + 1) * PAIR)
            probs.append(dict(
                sub=sub, p=p,
                la=stack(at[:, ps]).astype(BF16), lr=stack(rt[:, ps]), lb=stack(bt[:, ps]).astype(BF16),
                lk=stack(kt[:, ps]).astype(BF16), lbh=stack(bh[:, ps]), lkh=stack(kh[:, ps]), lv=stack(vc[:, ps]),
                e_end=e_end[:, ps]))
    for pr in probs:
        pr["lrb"] = pr["lr"].astype(BF16)
        pr["lvb"] = pr["lv"].astype(BF16)
        sc = _dot_nt(jnp.concatenate([pr["la"], pr["lrb"]], axis=0),
                     jnp.concatenate([pr["lb"], pr["lk"]], axis=0))
        pr["pw"] = jnp.where(strict, sc[0:r2, 0:r2], 0.0)
        pr["dak"] = jnp.where(strict, sc[0:r2, r2:2 * r2], 0.0)
        pr["drb"] = jnp.where(incl, sc[r2:2 * r2, 0:r2], 0.0).astype(BF16)
        pr["drk"] = jnp.where(incl, sc[r2:2 * r2, r2:2 * r2], 0.0)
        pr["inv"] = eye + pr["pw"]
    for step in range(nsteps):
        for pr in probs:
            pb = pr["pw"].astype(BF16)
            if step == 0:
                if nsteps > 1:
                    pr["pw"] = _dot(pb, pb)
            elif step < nsteps - 1:
                px = _dot(pb, jnp.concatenate([pb, pr["inv"].astype(BF16)], axis=1))
                pr["pw"] = px[:, 0:r2]
                pr["inv"] = pr["inv"] + px[:, r2:2 * r2]
            else:
                pr["inv"] = pr["inv"] + _dot(pb, pr["inv"].astype(BF16))
    for pr in probs:
        pr["dv"] = _dot(jnp.concatenate([pr["dak"], pr["drk"]], axis=0).astype(BF16), pr["lvb"])
    for pr in probs:
        tw = _dot(pr["inv"].astype(BF16), jnp.concatenate([pr["la"], pr["dv"][0:r2].astype(BF16)], axis=1))
        pr["lw"] = tw[:, 0:PAIR]
        pr["lu0"] = tw[:, PAIR:2 * PAIR]

    y_chunks = []
    for sub in range(nsub):
        sub_probs = [pr for pr in probs if pr["sub"] == sub]
        for pr in sub_probs:
            pr["ws"] = []
            for q in range(nseq):
                key = (0 if carry else sub * nseq + q, pr["p"])
                lhs = jnp.concatenate([seq_rows(pr["lw"], q), seq_rows(pr["lr"], q)], axis=0).astype(BF16)
                pr["ws"].append(_dot_nt(lhs, state[key].astype(BF16)))
        for pr in sub_probs:
            lu = unseq([w[0:2 * tlen] for w in pr["ws"]]) + pr["lu0"]
            pr["lu"] = lu
            y2 = unseq([w[2 * tlen:4 * tlen] for w in pr["ws"]]) + _dot(pr["drb"], lu.astype(BF16)) + pr["dv"][r2:2 * r2]
            pr["y"] = y2[0:rows] + y2[rows:r2]
        for pr in sub_probs:
            for q in range(nseq):
                key = (0 if carry else sub * nseq + q, pr["p"])
                lhs = jnp.concatenate([seq_rows(pr["lu"], q), seq_rows(pr["lv"], q)], axis=0).astype(BF16)
                rhs = jnp.concatenate([seq_rows(pr["lbh"], q), seq_rows(pr["lkh"], q)], axis=0).astype(BF16)
                decay = pr["e_end"][q * tlen:q * tlen + 1, :]
                state[key] = state[key] * decay + _dot_tn(lhs, rhs)
        y_chunks.append(_cat([pr["y"] for pr in sub_probs], axis=-1))
    y = _cat(y_chunks, axis=0)

    inv_n = 1.0 / HEAD_DIM
    mean = _head_sum(y) * inv_n
    d = y - mean
    var = _head_sum(d * d) * inv_n
    yn = d * lax.rsqrt(var + GN_EPS) * lnw_ref[...] + lnb_ref[...]
    bonus = _head_sum(r * k2 * bonus_ref[...]) * v
    out_ref[...] = (yn + bonus) * _silu(gate_ref[...])

    def write_state(q, src):
        for p in range(npair):
            sp = src[(q, p)]
            sout_ref[q, 2 * p] = sp[0:HEAD_DIM, 0:HEAD_DIM]
            sout_ref[q, 2 * p + 1] = sp[HEAD_DIM:PAIR, HEAD_DIM:PAIR]

    if carry:
        for p in range(npair):
            s_scr[p] = state[(0, p)]

        @pl.when(tb == nblk - 1)
        def _():
            write_state(0, state)
    else:
        for q in range(nsub * nseq):
            write_state(q, state)


def _rwkv(cols, gate, prev, s0, lp, *, nseq_total, rows, tlen, nsub, t_valid, carry):
    m, ncols = cols.shape
    width = gate.shape[1]
    blk_rows = rows * nsub
    heads = width // HEAD_DIM
    const = lambda b, t: (0, 0)
    vec = lambda n: pl.BlockSpec((1, n), const)
    if carry:
        nblk = m // nseq_total // blk_rows
        grid = (nseq_total, nblk)
        blk = lambda b, t: (b * nblk + t, 0)
        prev_arr = prev.reshape(nseq_total, 1, ncols)
        prev_spec = pl.BlockSpec((1, 1, ncols), lambda b, t: (b, 0, 0))
        sblk = 1
    else:
        grid = (m // blk_rows, 1)
        blk = lambda b, t: (b, 0)
        prev_arr = prev
        prev_spec = pl.BlockSpec((blk_rows, ncols), blk)
        sblk = blk_rows // tlen
    state_spec = pl.BlockSpec((sblk, heads, HEAD_DIM, HEAD_DIM), lambda b, t: (b, 0, 0, 0))
    return pl.pallas_call(
        functools.partial(_rwkv_kernel, rows=rows, tlen=tlen, nsub=nsub, width=width, t_valid=t_valid, carry=carry),
        grid=grid,
        in_specs=[pl.BlockSpec((blk_rows, ncols), blk), pl.BlockSpec((blk_rows, width), blk), prev_spec, state_spec,
                  vec(ncols), vec(width), pl.BlockSpec((PAIR, 2 * width), const), vec(width), vec(width),
                  vec(width), vec(width), vec(width), vec(width)],
        out_specs=[pl.BlockSpec((blk_rows, width), blk), state_spec],
        out_shape=[jax.ShapeDtypeStruct((m, width), F32),
                   jax.ShapeDtypeStruct((nseq_total, heads, HEAD_DIM, HEAD_DIM), F32)],
        scratch_shapes=[pltpu.VMEM((width // PAIR, PAIR, PAIR), F32), pltpu.VMEM((1, ncols), F32)],
        compiler_params=_params(("arbitrary", "arbitrary")),
        name="rwkv_prompt" if carry else "rwkv_sample",
    )(cols, gate, prev_arr, s0, lp["mu"], lp["w0"], lp["lora"], lp["a0"], lp["key_k"],
      lp["key_a"], lp["bonus"], lp["lnx_w"], lp["lnx_b"])


def _lambda(lam_ref, lam_init):
    lam = lam_ref[...]
    e1 = jnp.exp(jnp.sum(lam[0:1] * lam[1:2], axis=-1, keepdims=True))
    e2 = jnp.exp(jnp.sum(lam[2:3] * lam[3:4], axis=-1, keepdims=True))
    return e1 - e2 + lam_init


def _subln_gate(o, subw, gate, lam_init):
    o = o * lax.rsqrt(jnp.mean(o * o, axis=-1, keepdims=True) + SUBLN_EPS) * subw
    return o * (1.0 - lam_init) * _silu(gate)


def _attn_prompt_kernel(q_ref, k_ref, vt_ref, dg_ref, subw_ref, lam_ref, out_ref, *, tq, lam_init):
    seq = q_ref.shape[0]
    lo = lax.broadcasted_iota(jnp.int32, (tq, PAIR), 1) < HEAD_DIM
    ki = lax.broadcasted_iota(jnp.int32, (tq, 2 * tq), 0)
    qi = lax.broadcasted_iota(jnp.int32, (tq, 2 * tq), 1)
    causal = ki <= jnp.where(qi >= tq, qi - tq, qi)
    lam = _lambda(lam_ref, lam_init)
    for i in range(seq // tq):
        past, here = i * tq, (i + 1) * tq
        q = q_ref[past:here, :] * BF16(1.0 / math.sqrt(HEAD_DIM))
        zero = jnp.zeros_like(q)
        qs = jnp.concatenate([jnp.where(lo, q, zero), jnp.where(lo, zero, q)], axis=0)
        s_diag = jnp.where(causal, _dot_nt(k_ref[past:here, :], qs), -jnp.inf)
        m = jnp.max(s_diag, axis=0, keepdims=True)
        if i > 0:
            s_past = _dot_nt(k_ref[0:past, :], qs)
            m = jnp.maximum(m, jnp.max(s_past, axis=0, keepdims=True))
        p_diag = jnp.exp(s_diag - m)
        l = jnp.sum(p_diag, axis=0, keepdims=True)
        acc = _dot(vt_ref[:, past:here], p_diag.astype(BF16))
        if i > 0:
            p_past = jnp.exp(s_past - m)
            l = l + jnp.sum(p_past, axis=0, keepdims=True)
            acc = acc + _dot(vt_ref[:, 0:past], p_past.astype(BF16))
        ot = acc / l
        ot = ot[:, 0:tq] - lam * ot[:, tq:2 * tq]
        out_ref[past:here, :] = _subln_gate(ot.T, subw_ref[...], dg_ref[past:here, :], lam_init)


def _attn_prompt(q, kb, vt, dg, subw, lam_vecs, *, batch, seq, tq, lam_init):
    m, width = q.shape
    blk = pl.BlockSpec((seq, PAIR), lambda b, h: (b, h))
    const = lambda b, h: (0, 0)
    return pl.pallas_call(
        functools.partial(_attn_prompt_kernel, tq=tq, lam_init=lam_init),
        grid=(batch, width // PAIR),
        in_specs=[blk, blk, pl.BlockSpec((None, PAIR, seq), lambda b, h: (b, h, 0)), blk,
                  pl.BlockSpec((1, PAIR), const), pl.BlockSpec((4, HEAD_DIM), const)],
        out_specs=blk,
        out_shape=jax.ShapeDtypeStruct((m, width), F32),
        compiler_params=_params(("arbitrary", "arbitrary")),
        name="attn_prompt",
    )(q, kb, vt, dg, subw, lam_vecs)


def _attn_sample_kernel(pt_ref, q_ref, kn_ref, vn_ref, dg_ref, subw_ref, lam_ref, *rest,
                        npages, tpad, heads, lam_init):
    del pt_ref
    kp = rest[0:npages]
    vp = rest[npages:2 * npages]
    out_ref = rest[2 * npages]
    prow = kp[0].shape[0]
    nrow = 2 * heads * tpad
    half = nrow // 2
    q = q_ref[0] * (1.0 / math.sqrt(HEAD_DIM))
    lo = lax.broadcasted_iota(jnp.int32, (tpad, PAIR), 1) < HEAD_DIM
    pieces = []
    for c in range(2):
        for h in range(heads):
            qh = q[:, h * PAIR:(h + 1) * PAIR]
            pieces.append(jnp.where(lo, qh, 0.0) if c == 0 else jnp.where(lo, 0.0, qh))
    qst = jnp.concatenate(pieces, axis=0).astype(BF16)

    def two(refs, j):
        return jnp.concatenate([refs[j][...].astype(BF16), refs[j + 1][...].astype(BF16)], axis=0)

    rhead = (lax.broadcasted_iota(jnp.int32, (nrow, 2 * prow), 0) // tpad) % heads
    chead = lax.broadcasted_iota(jnp.int32, (nrow, 2 * prow), 1) % heads
    own = rhead == chead
    scores = [jnp.where(own, _dot_nt(qst, two(kp, j)), -jnp.inf) for j in range(0, npages, 2)]

    def new_rows(ref):
        x = ref[0]
        xs = [x[:, h * PAIR:(h + 1) * PAIR] for h in range(heads)]
        xs.append(jnp.zeros((LANES - heads * tpad, PAIR), F32))
        return jnp.concatenate(xs, axis=0).astype(BF16)

    s_new = _dot_nt(qst, new_rows(kn_ref))
    rn = lax.broadcasted_iota(jnp.int32, (nrow, LANES), 0)
    cn = lax.broadcasted_iota(jnp.int32, (nrow, LANES), 1)
    visible = ((rn // tpad) % heads == cn // tpad) & (cn % tpad <= rn % tpad) & (cn < heads * tpad)
    s_new = jnp.where(visible, s_new, -jnp.inf)
    m = jnp.max(s_new, axis=-1, keepdims=True)
    for s in scores:
        m = jnp.maximum(m, jnp.max(s, axis=-1, keepdims=True))
    p_new = jnp.exp(s_new - m)
    l = jnp.sum(p_new, axis=-1, keepdims=True)
    probs = []
    for s in scores:
        p = jnp.exp(s - m)
        l = l + jnp.sum(p, axis=-1, keepdims=True)
        probs.append(p)
    inv_l = 1.0 / l
    w1 = inv_l[0:half]
    w2 = _lambda(lam_ref, lam_init) * inv_l[half:nrow]

    def mix(p):
        return (p[0:half] * w1 - p[half:nrow] * w2).astype(BF16)

    o = _dot(mix(p_new), new_rows(vn_ref))
    for idx, j in enumerate(range(0, npages, 2)):
        o = o + _dot(mix(probs[idx]), two(vp, j))
    dg = dg_ref[0]
    outs = [_subln_gate(o[h * tpad:(h + 1) * tpad], subw_ref[...], dg[:, h * PAIR:(h + 1) * PAIR], lam_init)
            for h in range(heads)]
    out_ref[0] = jnp.concatenate(outs, axis=-1)


def _attn_sample(q, kn, vn, dg, subw, lam_vecs, cache_k, cache_v, page_table, *, layer, lam_init):
    batch, tpad, width = q.shape
    npages = page_table.shape[1]
    n_phys, page, heads = cache_k.shape[1], cache_k.shape[2], cache_k.shape[3]
    prow = page * heads
    ck = cache_k.reshape(-1, PAIR)
    cv = cache_v.reshape(-1, PAIR)
    pt = page_table.reshape(-1).astype(jnp.int32) + layer * n_phys
    tok = pl.BlockSpec((1, tpad, width), lambda b, pt: (b, 0, 0))
    const = lambda b, pt: (0, 0)

    def page_spec(j):
        return pl.BlockSpec((prow, PAIR), lambda b, pt: (pt[b * npages + j], 0))

    grid_spec = pltpu.PrefetchScalarGridSpec(
        num_scalar_prefetch=1,
        grid=(batch,),
        in_specs=[tok, tok, tok, tok, pl.BlockSpec((1, PAIR), const), pl.BlockSpec((4, HEAD_DIM), const)]
        + [page_spec(j) for j in range(npages)] + [page_spec(j) for j in range(npages)],
        out_specs=tok,
    )
    return pl.pallas_call(
        functools.partial(_attn_sample_kernel, npages=npages, tpad=tpad, heads=heads, lam_init=lam_init),
        grid_spec=grid_spec,
        out_shape=jax.ShapeDtypeStruct((batch, tpad, width), F32),
        compiler_params=_params(("arbitrary",)),
        name="attn_sample",
    )(pt, q, kn, vn, dg, subw, lam_vecs, *([ck] * npages), *([cv] * npages))


def _outproj_kernel(x_ref, r_ref, d_ref, w_ref, fw_ref, o_ref, *, final):
    half = r_ref.shape[1]
    y = x_ref[...] + _dot(r_ref[...].astype(BF16), w_ref[0:half, :]) + _dot(d_ref[...].astype(BF16), w_ref[half:, :])
    if final:
        y = y * lax.rsqrt(jnp.mean(y * y, axis=-1, keepdims=True) + RMS_EPS) * fw_ref[...]
    o_ref[...] = y


def _outproj(x2d, rw, df, w_bf16, final_w, *, final, tm):
    m, d = x2d.shape
    half = rw.shape[1]
    row = lambda i: (i, 0)
    const = lambda i: (0, 0)
    return pl.pallas_call(
        functools.partial(_outproj_kernel, final=final),
        grid=(m // tm,),
        in_specs=[pl.BlockSpec((tm, d), row), pl.BlockSpec((tm, half), row), pl.BlockSpec((tm, half), row),
                  pl.BlockSpec(w_bf16.shape, const), pl.BlockSpec((1, d), const)],
        out_specs=pl.BlockSpec((tm, d), row),
        out_shape=jax.ShapeDtypeStruct((m, d), F32),
        compiler_params=_params(("arbitrary",)),
        name="outproj",
    )(x2d, rw, df, w_bf16, final_w.reshape(1, d))


def kernel(x_prompt, x_sample, cache_k, cache_v, page_table, state_rwkv, state_shift, norm_w, w_in, shift_mu, decay_w0, decay_lora_b, iclr_a0, iclr_lora_b, key_k, key_a, bonus_r_k, lnx_w, lnx_b, lam_q1, lam_k1, lam_q2, lam_k2, subln_w, w_out, final_norm_w):
    depth = w_in.shape[0]
    bp, seq, d = x_prompt.shape
    bs, tdec, _ = x_sample.shape
    width = decay_w0.shape[-1]
    ncols = shift_mu.shape[-1]
    rank_w, rank_a = decay_lora_b.shape[1], iclr_lora_b.shape[1]
    heads_r = width // HEAD_DIM
    heads_d = cache_k.shape[3]
    dwidth = heads_d * cache_v.shape[4]
    dqk = heads_d * cache_k.shape[4]
    assert rank_w == HEAD_DIM and rank_a == HEAD_DIM and ncols == 3 * width + PAIR
    assert cache_k.shape[4] == PAIR and cache_v.shape[4] == PAIR
    c0 = ncols
    c1 = c0 + width
    c2 = c1 + dqk
    c3 = c2 + dqk
    c4 = c3 + dwidth
    c5 = c4 + dwidth
    bounds = (c0, c1, c2, c3, c4, c5)
    assert w_in.shape[2] == c5

    tpad = SUBLANES
    chunk = 64
    tile = 256
    assert tdec <= tpad
    xp = x_prompt.reshape(bp * seq, d)
    xs = jnp.pad(x_sample, ((0, 0), (0, tpad - tdec), (0, 0))).reshape(bs * tpad, d)

    outs = {n: [] for n in ("kp", "vp", "sp", "hp", "ks", "vs", "ss", "hs")}
    for l in range(depth):
        lam_init = 0.8 - 0.6 * math.exp(-0.3 * l)
        w_l = w_in[l].astype(BF16)
        wo_l = w_out[l].astype(BF16)
        zeros = jnp.zeros((rank_w, width), F32)
        lora = jnp.concatenate([jnp.concatenate([decay_lora_b[l], zeros], axis=1),
                                jnp.concatenate([zeros, iclr_lora_b[l]], axis=1)], axis=0).astype(BF16)
        lp = dict(mu=shift_mu[l].reshape(1, ncols), w0=decay_w0[l].reshape(1, width), lora=lora,
                  a0=iclr_a0[l].reshape(1, width), key_k=key_k[l].reshape(1, width), key_a=key_a[l].reshape(1, width),
                  bonus=bonus_r_k[l].reshape(1, width), lnx_w=lnx_w[l].reshape(1, width),
                  lnx_b=lnx_b[l].reshape(1, width))
        lam_vecs = jnp.stack([lam_q1[l], lam_k1[l], lam_q2[l], lam_k2[l]], axis=0)
        subw = subln_w[l].reshape(1, PAIR)
        last = l == depth - 1

        cols, rg, q, k, v, kb, vt, dg = _proj(xp, norm_w[l], w_l, bounds, tm=tile, prompt=True, seq=seq)
        rw, s_new = _rwkv(cols, rg, jnp.zeros((bp, ncols), F32), jnp.zeros((bp, heads_r, HEAD_DIM, HEAD_DIM), F32),
                          lp, nseq_total=bp, rows=chunk, tlen=chunk, nsub=2, t_valid=chunk, carry=True)
        df = _attn_prompt(q, kb, vt, dg, subw, lam_vecs, batch=bp, seq=seq, tq=tile, lam_init=lam_init)
        xp = _outproj(xp, rw, df, wo_l, final_norm_w, final=last, tm=tile)
        outs["kp"].append(k.reshape(bp, seq, heads_d, PAIR))
        outs["vp"].append(v.reshape(bp, seq, heads_d, PAIR))
        outs["sp"].append(s_new)
        outs["hp"].append(cols.reshape(bp, seq, ncols)[:, -1, :])

        cols, rg, q, k, v, dg = _proj(xs, norm_w[l], w_l, bounds, tm=tile, prompt=False)
        prev = jnp.pad(state_shift[l][:, None, :], ((0, 0), (0, tpad - 1), (0, 0))).reshape(bs * tpad, ncols)
        rw, s_new = _rwkv(cols, rg, prev, state_rwkv[l], lp, nseq_total=bs, rows=chunk, tlen=tpad, nsub=1,
                          t_valid=tdec, carry=False)
        r3 = lambda t: t.reshape(bs, tpad, -1)
        df = _attn_sample(r3(q), r3(k), r3(v), r3(dg), subw, lam_vecs, cache_k, cache_v, page_table,
                          layer=l, lam_init=lam_init)
        xs = _outproj(xs, rw, df.reshape(bs * tpad, dwidth), wo_l, final_norm_w, final=last, tm=tile)
        outs["ks"].append(r3(k)[:, :tdec].reshape(bs, tdec, heads_d, PAIR))
        outs["vs"].append(r3(v)[:, :tdec].reshape(bs, tdec, heads_d, PAIR))
        outs["ss"].append(s_new)
        outs["hs"].append(r3(cols)[:, tdec - 1, :])

    y_prompt = xp.reshape(bp, seq, d)
    y_sample = xs.reshape(bs, tpad, d)[:, :tdec]
    st = lambda n: jnp.stack(outs[n], axis=0)
    return (y_prompt, y_sample, st("kp"), st("vp"), st("sp"), st("hp"), st("ks"), st("vs"), st("ss"), st("hs"))
```

```python
import functools
import math

import jax
import jax.numpy as jnp
from jax import lax
from jax.experimental import pallas as pl
from jax.experimental.pallas import tpu as pltpu

F32 = jnp.float32
BF16 = jnp.bfloat16

RMS_EPS = 1e-6
SUBLN_EPS = 1e-5
GN_EPS = 64e-5

LANES = 128
SUBLANES = 8
HEAD_DIM = 64
PAIR = 2 * HEAD_DIM
VMEM_LIMIT = 48 * 1024 * 1024

_NT = (((1,), (1,)), ((), ()))
_TN = (((0,), (0,)), ((), ()))


def _dot(a, b):
    return jnp.dot(a, b, preferred_element_type=F32)


def _dot_nt(a, b):
    return lax.dot_general(a, b, _NT, preferred_element_type=F32)


def _dot_tn(a, b):
    return lax.dot_general(a, b, _TN, preferred_element_type=F32)


def _sigmoid(x):
    return 1.0 / (1.0 + jnp.exp(-x))


def _silu(x):
    return x * _sigmoid(x)


def _cat(xs, axis):
    return xs[0] if len(xs) == 1 else jnp.concatenate(xs, axis=axis)


def _params(sem):
    return pltpu.CompilerParams(dimension_semantics=sem, vmem_limit_bytes=VMEM_LIMIT)


def _store_token_head(dst, val, heads):
    rows = val.shape[0]
    for h in range(heads):
        dst[pl.ds(h, rows, stride=heads), :] = val[:, h * PAIR:(h + 1) * PAIR]


def _proj_kernel(x_ref, nw_ref, w_ref, *refs, bounds, prompt, layer):
    x = x_ref[...]
    h = x * lax.rsqrt(jnp.mean(x * x, axis=-1, keepdims=True) + RMS_EPS) * nw_ref[...]
    hb = h.astype(BF16)
    c0, c1, c2, c3, c4, c5 = bounds
    mm = lambda lo, hi: _dot(hb, w_ref[:, lo:hi])
    if prompt:
        cols_ref, rg_ref, q_ref, kb_ref, vt_ref, dg_ref, kst_ref, vst_ref = refs[-8:]
    else:
        cols_ref, rg_ref, q_ref, k_ref, v_ref, dg_ref = refs
    cols_ref[...] = mm(0, c0)
    rg_ref[...] = mm(c0, c1)
    q = mm(c1, c2)
    k = mm(c2, c3)
    v = mm(c3, c4)
    dg_ref[...] = mm(c4, c5)
    if not prompt:
        q_ref[...] = q
        k_ref[...] = k
        v_ref[...] = v
        return
    q_ref[...] = q.astype(BF16)
    kb_ref[...] = k.astype(BF16)
    vt_ref[0] = v.T.astype(BF16)
    heads = (c3 - c2) // PAIR
    if layer == 0:
        for slot in range(kst_ref.shape[0]):
            _store_token_head(kst_ref.at[slot], k, heads)
            _store_token_head(vst_ref.at[slot], v, heads)
    else:
        _store_token_head(kst_ref, k, heads)
        _store_token_head(vst_ref, v, heads)


def _proj(x2d, norm_w, w_bf16, bounds, *, tm, prompt, seq=None, layer=0, depth=1, stacked=()):
    m, d = x2d.shape
    c0, c1, c2, c3, c4, c5 = bounds
    row = lambda i: (i, 0)
    const = lambda i: (0, 0)
    wide = lambda w, dt: (pl.BlockSpec((tm, w), row), jax.ShapeDtypeStruct((m, w), dt))
    in_specs = [pl.BlockSpec((tm, d), row), pl.BlockSpec((1, d), const), pl.BlockSpec(w_bf16.shape, const)]
    aliases = {}
    if prompt:
        nt = seq // tm
        heads = (c3 - c2) // PAIR
        outs = [wide(c0, F32), wide(c1 - c0, F32), wide(c2 - c1, BF16), wide(c3 - c2, BF16),
                (pl.BlockSpec((1, c4 - c3, tm), lambda i: (i // nt, 0, i % nt)),
                 jax.ShapeDtypeStruct((m // seq, c4 - c3, seq), BF16)),
                wide(c5 - c4, F32)]
        if layer == 0:
            st_spec = pl.BlockSpec((depth, tm * heads, PAIR), lambda i: (0, i, 0))
        else:
            st_spec = pl.BlockSpec((None, tm * heads, PAIR), lambda i: (layer, i, 0))
            in_specs += [pl.BlockSpec(memory_space=pl.ANY)] * 2
            aliases = {3: 6, 4: 7}
        outs += [(st_spec, jax.ShapeDtypeStruct((depth, m * heads, PAIR), F32))] * 2
    else:
        outs = [wide(c0, F32), wide(c1 - c0, F32), wide(c2 - c1, F32), wide(c3 - c2, F32), wide(c4 - c3, F32),
                wide(c5 - c4, F32)]
    return pl.pallas_call(
        functools.partial(_proj_kernel, bounds=bounds, prompt=prompt, layer=layer),
        grid=(m // tm,),
        in_specs=in_specs,
        out_specs=[o[0] for o in outs],
        out_shape=[o[1] for o in outs],
        input_output_aliases=aliases,
        compiler_params=_params(("arbitrary",)),
        name="proj_prompt" if prompt else "proj_sample",
    )(x2d, norm_w.reshape(1, d), w_bf16, *stacked)


def _head_sum(x):
    rows, width = x.shape
    lo = lax.broadcasted_iota(jnp.int32, (rows, PAIR), 1) < HEAD_DIM
    out = []
    for p in range(width // PAIR):
        xp = x[:, p * PAIR:(p + 1) * PAIR]
        s_lo = jnp.sum(jnp.where(lo, xp, 0.0), axis=-1, keepdims=True)
        s_hi = jnp.sum(jnp.where(lo, 0.0, xp), axis=-1, keepdims=True)
        out.append(jnp.where(lo, s_lo, s_hi))
    return jnp.concatenate(out, axis=-1)


def _split3(x):
    h0 = x.astype(BF16)
    r1 = x - h0.astype(F32)
    h1 = r1.astype(BF16)
    h2 = (r1 - h1.astype(F32)).astype(BF16)
    return h0, h1, h2


def _blockdiag(s_even, s_odd):
    zero = jnp.zeros((HEAD_DIM, HEAD_DIM), F32)
    return jnp.concatenate([jnp.concatenate([s_even, zero], axis=1), jnp.concatenate([zero, s_odd], axis=1)], axis=0)


def _rwkv_kernel(cols_ref, gate_ref, prev_ref, s0_ref, mu_ref, w0_ref, lora_ref, a0_ref, keyk_ref, keya_ref,
                 bonus_ref, lnw_ref, lnb_ref, *refs, rows, tlen, nsub, width, t_valid, carry, nslot):
    out_ref, sout_ref, s_scr, prev_scr = refs[-4:]
    tb = pl.program_id(1)
    nblk = pl.num_programs(1)
    npair = width // PAIR
    nseq = rows // tlen
    r2 = 2 * rows
    blk_rows = rows * nsub

    c = cols_ref[...]
    row_t = lax.broadcasted_iota(jnp.int32, c.shape, 0) % tlen if not carry else None
    if carry:
        @pl.when(tb == 0)
        def _():
            prev_scr[...] = prev_ref[0]
            for p in range(npair):
                s_scr[p] = _blockdiag(s0_ref[0, 2 * p], s0_ref[0, 2 * p + 1])

        row = lax.broadcasted_iota(jnp.int32, c.shape, 0)
        shifted = jnp.where(row == 0, prev_scr[...], pltpu.roll(c, 1, 0))
        prev_scr[...] = c[blk_rows - 1:blk_rows, :]
        state = {(0, p): s_scr[p] for p in range(npair)}
    else:
        shifted = jnp.where(row_t == 0, prev_ref[...], pltpu.roll(c, 1, 0))
        state = {(q, p): _blockdiag(s0_ref[q, 2 * p], s0_ref[q, 2 * p + 1])
                 for q in range(nsub * nseq) for p in range(npair)}

    xs = c + (shifted - c) * mu_ref[...]
    r = xs[:, 0:width]
    k = xs[:, width:2 * width]
    v = xs[:, 2 * width:3 * width]
    xwa = xs[:, 3 * width:3 * width + PAIR]
    lo_rank = lax.broadcasted_iota(jnp.int32, xwa.shape, 1) < HEAD_DIM
    xwa = jnp.where(lo_rank, jnp.tanh(xwa), xwa)
    lora = _dot(xwa.astype(BF16), lora_ref[...])
    z = -(w0_ref[...] + lora[:, 0:width])
    softplus = jnp.maximum(z, 0.0) + jnp.log(1.0 + jnp.exp(-jnp.abs(z)))
    lw = -jnp.exp(-softplus - 0.5)
    a = _sigmoid(a0_ref[...] + lora[:, width:2 * width])
    kk = k * keyk_ref[...]
    kk = kk / jnp.maximum(jnp.sqrt(_head_sum(kk * kk)), 1e-12)
    k2 = k * (1.0 + (a - 1.0) * keya_ref[...])
    av = -kk
    bv = kk * a
    if t_valid < tlen:
        live = lax.broadcasted_iota(jnp.int32, lw.shape, 0) % tlen < t_valid
        lw = jnp.where(live, lw, 0.0)
        av = jnp.where(live, av, 0.0)
        bv = jnp.where(live, bv, 0.0)
        k2s = jnp.where(live, k2, 0.0)
        vs = jnp.where(live, v, 0.0)
    else:
        k2s, vs = k2, v

    ri = lax.broadcasted_iota(jnp.int32, (r2, r2), 0)
    ci = lax.broadcasted_iota(jnp.int32, (r2, r2), 1)
    same = (ri // tlen) == (ci // tlen)
    strict = same & (ci < ri)
    incl = same & (ci <= ri)
    eye = (ri == ci).astype(F32)
    cum_lower = incl[0:rows, 0:rows]
    cum = jnp.concatenate([cum_lower, same[0:rows, 0:rows]], axis=0).astype(F32).astype(BF16)
    lo = lax.broadcasted_iota(jnp.int32, (rows, PAIR), 1) < HEAD_DIM
    nsteps = max(1, (tlen - 1).bit_length())

    def stack(xp):
        return jnp.concatenate([jnp.where(lo, xp, 0.0), jnp.where(lo, 0.0, xp)], axis=0)

    def seq_rows(x, q):
        if nseq == 1:
            return x
        return jnp.concatenate([x[q * tlen:(q + 1) * tlen], x[rows + q * tlen:rows + (q + 1) * tlen]], axis=0)

    def unseq(parts):
        if nseq == 1:
            return parts[0]
        return jnp.concatenate([x[0:tlen] for x in parts] + [x[tlen:2 * tlen] for x in parts], axis=0)

    probs = []
    for sub in range(nsub):
        sl = slice(sub * rows, (sub + 1) * rows)
        lwc = lw[sl]
        g0, g1, g2 = _split3(lwc)
        gg = _dot(cum, g0) + _dot(cum, g1) + _dot(cum, g2)
        g = gg[0:rows]
        g_end = gg[rows:r2]
        eg = jnp.exp(g)
        eng = jnp.exp(-g)
        egc = jnp.exp(g_end - g)
        at = av[sl] * jnp.exp(g - lwc)
        rt = r[sl] * eg
        bt = bv[sl] * eng
        kt = k2s[sl] * eng
        bh = bv[sl] * egc
        kh = k2s[sl] * egc
        e_end = jnp.exp(g_end)
        vc = vs[sl]
        for p in range(npair):
            ps = slice(p * PAIR, (p + 1) * PAIR)
            probs.append(dict(
                sub=sub, p=p,
                la=stack(at[:, ps]).astype(BF16), lr=stack(rt[:, ps]), lb=stack(bt[:, ps]).astype(BF16),
                lk=stack(kt[:, ps]).astype(BF16), lbh=stack(bh[:, ps]), lkh=stack(kh[:, ps]), lv=stack(vc[:, ps]),
                e_end=e_end[:, ps]))
    for pr in probs:
        pr["lrb"] = pr["lr"].astype(BF16)
        pr["lvb"] = pr["lv"].astype(BF16)
        sc = _dot_nt(jnp.concatenate([pr["la"], pr["lrb"]], axis=0),
                     jnp.concatenate([pr["lb"], pr["lk"]], axis=0))
        pr["pw"] = jnp.where(strict, sc[0:r2, 0:r2], 0.0)
        pr["dak"] = jnp.where(strict, sc[0:r2, r2:2 * r2], 0.0)
        pr["drb"] = jnp.where(incl, sc[r2:2 * r2, 0:r2], 0.0).astype(BF16)
        pr["drk"] = jnp.where(incl, sc[r2:2 * r2, r2:2 * r2], 0.0)
        pr["inv"] = eye + pr["pw"]
    for step in range(nsteps):
        for pr in probs:
            pb = pr["pw"].astype(BF16)
            if step == 0:
                if nsteps > 1:
                    pr["pw"] = _dot(pb, pb)
            elif step < nsteps - 1:
                px = _dot(pb, jnp.concatenate([pb, pr["inv"].astype(BF16)], axis=1))
                pr["pw"] = px[:, 0:r2]
                pr["inv"] = pr["inv"] + px[:, r2:2 * r2]
            else:
                pr["inv"] = pr["inv"] + _dot(pb, pr["inv"].astype(BF16))
    for pr in probs:
        pr["dv"] = _dot(jnp.concatenate([pr["dak"], pr["drk"]], axis=0).astype(BF16), pr["lvb"])
    for pr in probs:
        tw = _dot(pr["inv"].astype(BF16), jnp.concatenate([pr["la"], pr["dv"][0:r2].astype(BF16)], axis=1))
        pr["lw"] = tw[:, 0:PAIR]
        pr["lu0"] = tw[:, PAIR:2 * PAIR]

    y_chunks = []
    for sub in range(nsub):
        sub_probs = [pr for pr in probs if pr["sub"] == sub]
        for pr in sub_probs:
            pr["ws"] = []
            for q in range(nseq):
                key = (0 if carry else sub * nseq + q, pr["p"])
                lhs = jnp.concatenate([seq_rows(pr["lw"], q), seq_rows(pr["lr"], q)], axis=0).astype(BF16)
                pr["ws"].append(_dot_nt(lhs, state[key].astype(BF16)))
        for pr in sub_probs:
            lu = unseq([w[0:2 * tlen] for w in pr["ws"]]) + pr["lu0"]
            pr["lu"] = lu
            y2 = unseq([w[2 * tlen:4 * tlen] for w in pr["ws"]]) + _dot(pr["drb"], lu.astype(BF16)) + pr["dv"][r2:2 * r2]
            pr["y"] = y2[0:rows] + y2[rows:r2]
        for pr in sub_probs:
            for q in range(nseq):
                key = (0 if carry else sub * nseq + q, pr["p"])
                lhs = jnp.concatenate([seq_rows(pr["lu"], q), seq_rows(pr["lv"], q)], axis=0).astype(BF16)
                rhs = jnp.concatenate([seq_rows(pr["lbh"], q), seq_rows(pr["lkh"], q)], axis=0).astype(BF16)
                decay = pr["e_end"][q * tlen:q * tlen + 1, :]
                state[key] = state[key] * decay + _dot_tn(lhs, rhs)
        y_chunks.append(_cat([pr["y"] for pr in sub_probs], axis=-1))
    y = _cat(y_chunks, axis=0)

    inv_n = 1.0 / HEAD_DIM
    mean = _head_sum(y) * inv_n
    d = y - mean
    var = _head_sum(d * d) * inv_n
    yn = d * lax.rsqrt(var + GN_EPS) * lnw_ref[...] + lnb_ref[...]
    bonus = _head_sum(r * k2 * bonus_ref[...]) * v
    out_ref[...] = (yn + bonus) * _silu(gate_ref[...])

    views = [sout_ref.at[s] for s in range(sout_ref.shape[0])] if nslot else [sout_ref]

    def write_state(q, src):
        for p in range(npair):
            sp = src[(q, p)]
            for dst in views:
                dst[q, 2 * p] = sp[0:HEAD_DIM, 0:HEAD_DIM]
                dst[q, 2 * p + 1] = sp[HEAD_DIM:PAIR, HEAD_DIM:PAIR]

    if carry:
        for p in range(npair):
            s_scr[p] = state[(0, p)]

        @pl.when(tb == nblk - 1)
        def _():
            write_state(0, state)
    else:
        for q in range(nsub * nseq):
            write_state(q, state)


def _rwkv(cols, gate, prev, s0, lp, *, nseq_total, rows, tlen, nsub, t_valid, carry, layer=0, depth=1, stacked=()):
    m, ncols = cols.shape
    width = gate.shape[1]
    blk_rows = rows * nsub
    heads = width // HEAD_DIM
    const = lambda b, t: (0, 0)
    vec = lambda n: pl.BlockSpec((1, n), const)
    hd = (heads, HEAD_DIM, HEAD_DIM)
    aliases = {}
    extra_specs = []
    nslot = 0
    if carry:
        nblk = m // nseq_total // blk_rows
        grid = (nseq_total, nblk)
        blk = lambda b, t: (b * nblk + t, 0)
        prev_arr = prev.reshape(nseq_total, 1, ncols)
        prev_spec = pl.BlockSpec((1, 1, ncols), lambda b, t: (b, 0, 0))
        s0_spec = sout_spec = pl.BlockSpec((1,) + hd, lambda b, t: (b, 0, 0, 0))
        sout_shape = (nseq_total,) + hd
    else:
        nstep = m // blk_rows
        grid = (nstep, 1)
        blk = lambda b, t: (b, 0)
        prev_arr = prev
        prev_spec = pl.BlockSpec((blk_rows, ncols), blk)
        sblk = blk_rows // tlen
        s0_spec = pl.BlockSpec((sblk,) + hd, lambda b, t: (layer * nstep + b, 0, 0, 0))
        sout_shape = (depth, nseq_total) + hd
        if layer == 0:
            nslot = depth
            sout_spec = pl.BlockSpec((depth, sblk) + hd, lambda b, t: (0, b, 0, 0, 0))
        else:
            sout_spec = pl.BlockSpec((None, sblk) + hd, lambda b, t: (layer, b, 0, 0, 0))
            extra_specs = [pl.BlockSpec(memory_space=pl.ANY)]
            aliases = {13: 1}
    return pl.pallas_call(
        functools.partial(_rwkv_kernel, rows=rows, tlen=tlen, nsub=nsub, width=width, t_valid=t_valid, carry=carry,
                          nslot=nslot),
        grid=grid,
        in_specs=[pl.BlockSpec((blk_rows, ncols), blk), pl.BlockSpec((blk_rows, width), blk), prev_spec, s0_spec,
                  vec(ncols), vec(width), pl.BlockSpec((PAIR, 2 * width), const), vec(width), vec(width),
                  vec(width), vec(width), vec(width), vec(width)] + extra_specs,
        out_specs=[pl.BlockSpec((blk_rows, width), blk), sout_spec],
        out_shape=[jax.ShapeDtypeStruct((m, width), F32), jax.ShapeDtypeStruct(sout_shape, F32)],
        input_output_aliases=aliases,
        scratch_shapes=[pltpu.VMEM((width // PAIR, PAIR, PAIR), F32), pltpu.VMEM((1, ncols), F32)],
        compiler_params=_params(("arbitrary", "arbitrary")),
        name="rwkv_prompt" if carry else "rwkv_sample",
    )(cols, gate, prev_arr, s0, lp["mu"], lp["w0"], lp["lora"], lp["a0"], lp["key_k"],
      lp["key_a"], lp["bonus"], lp["lnx_w"], lp["lnx_b"], *stacked)


def _lambda(lam_ref, lam_init):
    lam = lam_ref[...]
    e1 = jnp.exp(jnp.sum(lam[0:1] * lam[1:2], axis=-1, keepdims=True))
    e2 = jnp.exp(jnp.sum(lam[2:3] * lam[3:4], axis=-1, keepdims=True))
    return e1 - e2 + lam_init


def _subln_gate(o, subw, gate, lam_init):
    o = o * lax.rsqrt(jnp.mean(o * o, axis=-1, keepdims=True) + SUBLN_EPS) * subw
    return o * (1.0 - lam_init) * _silu(gate)


def _attn_prompt_kernel(q_ref, k_ref, vt_ref, dg_ref, subw_ref, lam_ref, out_ref, *, tq, lam_init):
    seq = q_ref.shape[0]
    lo = lax.broadcasted_iota(jnp.int32, (tq, PAIR), 1) < HEAD_DIM
    ki = lax.broadcasted_iota(jnp.int32, (tq, 2 * tq), 0)
    qi = lax.broadcasted_iota(jnp.int32, (tq, 2 * tq), 1)
    causal = ki <= jnp.where(qi >= tq, qi - tq, qi)
    lam = _lambda(lam_ref, lam_init)
    for i in range(seq // tq):
        past, here = i * tq, (i + 1) * tq
        q = q_ref[past:here, :] * BF16(1.0 / math.sqrt(HEAD_DIM))
        zero = jnp.zeros_like(q)
        qs = jnp.concatenate([jnp.where(lo, q, zero), jnp.where(lo, zero, q)], axis=0)
        s_diag = jnp.where(causal, _dot_nt(k_ref[past:here, :], qs), -jnp.inf)
        m = jnp.max(s_diag, axis=0, keepdims=True)
        if i > 0:
            s_past = _dot_nt(k_ref[0:past, :], qs)
            m = jnp.maximum(m, jnp.max(s_past, axis=0, keepdims=True))
        p_diag = jnp.exp(s_diag - m)
        l = jnp.sum(p_diag, axis=0, keepdims=True)
        acc = _dot(vt_ref[:, past:here], p_diag.astype(BF16))
        if i > 0:
            p_past = jnp.exp(s_past - m)
            l = l + jnp.sum(p_past, axis=0, keepdims=True)
            acc = acc + _dot(vt_ref[:, 0:past], p_past.astype(BF16))
        ot = acc / l
        ot = ot[:, 0:tq] - lam * ot[:, tq:2 * tq]
        out_ref[past:here, :] = _subln_gate(ot.T, subw_ref[...], dg_ref[past:here, :], lam_init)


def _attn_prompt(q, kb, vt, dg, subw, lam_vecs, *, batch, seq, tq, lam_init):
    m, width = q.shape
    blk = pl.BlockSpec((seq, PAIR), lambda b, h: (b, h))
    const = lambda b, h: (0, 0)
    return pl.pallas_call(
        functools.partial(_attn_prompt_kernel, tq=tq, lam_init=lam_init),
        grid=(batch, width // PAIR),
        in_specs=[blk, blk, pl.BlockSpec((None, PAIR, seq), lambda b, h: (b, h, 0)), blk,
                  pl.BlockSpec((1, PAIR), const), pl.BlockSpec((4, HEAD_DIM), const)],
        out_specs=blk,
        out_shape=jax.ShapeDtypeStruct((m, width), F32),
        compiler_params=_params(("arbitrary", "arbitrary")),
        name="attn_prompt",
    )(q, kb, vt, dg, subw, lam_vecs)


def _attn_sample_kernel(pt_ref, q_ref, kn_ref, vn_ref, dg_ref, subw_ref, lam_ref, *rest,
                        npages, tpad, heads, lam_init):
    del pt_ref
    kp = rest[0:npages]
    vp = rest[npages:2 * npages]
    out_ref = rest[2 * npages]
    prow = kp[0].shape[0]
    nrow = 2 * heads * tpad
    half = nrow // 2
    q = q_ref[0] * (1.0 / math.sqrt(HEAD_DIM))
    lo = lax.broadcasted_iota(jnp.int32, (tpad, PAIR), 1) < HEAD_DIM
    pieces = []
    for c in range(2):
        for h in range(heads):
            qh = q[:, h * PAIR:(h + 1) * PAIR]
            pieces.append(jnp.where(lo, qh, 0.0) if c == 0 else jnp.where(lo, 0.0, qh))
    qst = jnp.concatenate(pieces, axis=0).astype(BF16)

    def two(refs, j):
        return jnp.concatenate([refs[j][...].astype(BF16), refs[j + 1][...].astype(BF16)], axis=0)

    rhead = (lax.broadcasted_iota(jnp.int32, (nrow, 2 * prow), 0) // tpad) % heads
    chead = lax.broadcasted_iota(jnp.int32, (nrow, 2 * prow), 1) % heads
    own = rhead == chead
    scores = [jnp.where(own, _dot_nt(qst, two(kp, j)), -jnp.inf) for j in range(0, npages, 2)]

    def new_rows(ref):
        x = ref[0]
        xs = [x[:, h * PAIR:(h + 1) * PAIR] for h in range(heads)]
        xs.append(jnp.zeros((LANES - heads * tpad, PAIR), F32))
        return jnp.concatenate(xs, axis=0).astype(BF16)

    s_new = _dot_nt(qst, new_rows(kn_ref))
    rn = lax.broadcasted_iota(jnp.int32, (nrow, LANES), 0)
    cn = lax.broadcasted_iota(jnp.int32, (nrow, LANES), 1)
    visible = ((rn // tpad) % heads == cn // tpad) & (cn % tpad <= rn % tpad) & (cn < heads * tpad)
    s_new = jnp.where(visible, s_new, -jnp.inf)
    m = jnp.max(s_new, axis=-1, keepdims=True)
    for s in scores:
        m = jnp.maximum(m, jnp.max(s, axis=-1, keepdims=True))
    p_new = jnp.exp(s_new - m)
    l = jnp.sum(p_new, axis=-1, keepdims=True)
    probs = []
    for s in scores:
        p = jnp.exp(s - m)
        l = l + jnp.sum(p, axis=-1, keepdims=True)
        probs.append(p)
    inv_l = 1.0 / l
    w1 = inv_l[0:half]
    w2 = _lambda(lam_ref, lam_init) * inv_l[half:nrow]

    def mix(p):
        return (p[0:half] * w1 - p[half:nrow] * w2).astype(BF16)

    o = _dot(mix(p_new), new_rows(vn_ref))
    for idx, j in enumerate(range(0, npages, 2)):
        o = o + _dot(mix(probs[idx]), two(vp, j))
    dg = dg_ref[0]
    outs = [_subln_gate(o[h * tpad:(h + 1) * tpad], subw_ref[...], dg[:, h * PAIR:(h + 1) * PAIR], lam_init)
            for h in range(heads)]
    out_ref[0] = jnp.concatenate(outs, axis=-1)


def _attn_sample(q, kn, vn, dg, subw, lam_vecs, cache_k, cache_v, page_table, *, layer, lam_init):
    batch, tpad, width = q.shape
    npages = page_table.shape[1]
    n_phys, page, heads = cache_k.shape[1], cache_k.shape[2], cache_k.shape[3]
    prow = page * heads
    ck = cache_k.reshape(-1, PAIR)
    cv = cache_v.reshape(-1, PAIR)
    pt = page_table.reshape(-1).astype(jnp.int32) + layer * n_phys
    tok = pl.BlockSpec((1, tpad, width), lambda b, pt: (b, 0, 0))
    const = lambda b, pt: (0, 0)

    def page_spec(j):
        return pl.BlockSpec((prow, PAIR), lambda b, pt: (pt[b * npages + j], 0))

    grid_spec = pltpu.PrefetchScalarGridSpec(
        num_scalar_prefetch=1,
        grid=(batch,),
        in_specs=[tok, tok, tok, tok, pl.BlockSpec((1, PAIR), const), pl.BlockSpec((4, HEAD_DIM), const)]
        + [page_spec(j) for j in range(npages)] + [page_spec(j) for j in range(npages)],
        out_specs=tok,
    )
    return pl.pallas_call(
        functools.partial(_attn_sample_kernel, npages=npages, tpad=tpad, heads=heads, lam_init=lam_init),
        grid_spec=grid_spec,
        out_shape=jax.ShapeDtypeStruct((batch, tpad, width), F32),
        compiler_params=_params(("arbitrary",)),
        name="attn_sample",
    )(pt, q, kn, vn, dg, subw, lam_vecs, *([ck] * npages), *([cv] * npages))


def _outproj_kernel(x_ref, r_ref, d_ref, w_ref, fw_ref, o_ref, *, final):
    half = r_ref.shape[1]
    y = x_ref[...] + _dot(r_ref[...].astype(BF16), w_ref[0:half, :]) + _dot(d_ref[...].astype(BF16), w_ref[half:, :])
    if final:
        y = y * lax.rsqrt(jnp.mean(y * y, axis=-1, keepdims=True) + RMS_EPS) * fw_ref[...]
    o_ref[...] = y


def _outproj(x2d, rw, df, w_bf16, final_w, *, final, tm):
    m, d = x2d.shape
    half = rw.shape[1]
    row = lambda i: (i, 0)
    const = lambda i: (0, 0)
    return pl.pallas_call(
        functools.partial(_outproj_kernel, final=final),
        grid=(m // tm,),
        in_specs=[pl.BlockSpec((tm, d), row), pl.BlockSpec((tm, half), row), pl.BlockSpec((tm, half), row),
                  pl.BlockSpec(w_bf16.shape, const), pl.BlockSpec((1, d), const)],
        out_specs=pl.BlockSpec((tm, d), row),
        out_shape=jax.ShapeDtypeStruct((m, d), F32),
        compiler_params=_params(("arbitrary",)),
        name="outproj",
    )(x2d, rw, df, w_bf16, final_w.reshape(1, d))


def kernel(x_prompt, x_sample, cache_k, cache_v, page_table, state_rwkv, state_shift, norm_w, w_in, shift_mu, decay_w0, decay_lora_b, iclr_a0, iclr_lora_b, key_k, key_a, bonus_r_k, lnx_w, lnx_b, lam_q1, lam_k1, lam_q2, lam_k2, subln_w, w_out, final_norm_w):
    depth = w_in.shape[0]
    bp, seq, d = x_prompt.shape
    bs, tdec, _ = x_sample.shape
    width = decay_w0.shape[-1]
    ncols = shift_mu.shape[-1]
    rank_w, rank_a = decay_lora_b.shape[1], iclr_lora_b.shape[1]
    heads_r = width // HEAD_DIM
    heads_d = cache_k.shape[3]
    dwidth = heads_d * cache_v.shape[4]
    dqk = heads_d * cache_k.shape[4]
    assert rank_w == HEAD_DIM and rank_a == HEAD_DIM and ncols == 3 * width + PAIR
    assert cache_k.shape[4] == PAIR and cache_v.shape[4] == PAIR
    c0 = ncols
    c1 = c0 + width
    c2 = c1 + dqk
    c3 = c2 + dqk
    c4 = c3 + dwidth
    c5 = c4 + dwidth
    bounds = (c0, c1, c2, c3, c4, c5)
    assert w_in.shape[2] == c5

    tpad = SUBLANES
    chunk = 64
    tile = 256
    assert tdec <= tpad
    xp = x_prompt.reshape(bp * seq, d)
    xs = jnp.pad(x_sample, ((0, 0), (0, tpad - tdec), (0, 0))).reshape(bs * tpad, d)

    state_all = state_rwkv.reshape((depth * bs,) + state_rwkv.shape[2:])
    kst = vst = sst = None
    outs = {n: [] for n in ("sp", "hp", "ks", "vs", "hs")}
    for l in range(depth):
        lam_init = 0.8 - 0.6 * math.exp(-0.3 * l)
        w_l = w_in[l].astype(BF16)
        wo_l = w_out[l].astype(BF16)
        zeros = jnp.zeros((rank_w, width), F32)
        lora = jnp.concatenate([jnp.concatenate([decay_lora_b[l], zeros], axis=1),
                                jnp.concatenate([zeros, iclr_lora_b[l]], axis=1)], axis=0).astype(BF16)
        lp = dict(mu=shift_mu[l].reshape(1, ncols), w0=decay_w0[l].reshape(1, width), lora=lora,
                  a0=iclr_a0[l].reshape(1, width), key_k=key_k[l].reshape(1, width), key_a=key_a[l].reshape(1, width),
                  bonus=bonus_r_k[l].reshape(1, width), lnx_w=lnx_w[l].reshape(1, width),
                  lnx_b=lnx_b[l].reshape(1, width))
        lam_vecs = jnp.stack([lam_q1[l], lam_k1[l], lam_q2[l], lam_k2[l]], axis=0)
        subw = subln_w[l].reshape(1, PAIR)
        last = l == depth - 1

        cols, rg, q, kb, vt, dg, kst, vst = _proj(xp, norm_w[l], w_l, bounds, tm=tile, prompt=True, seq=seq,
                                                  layer=l, depth=depth, stacked=() if l == 0 else (kst, vst))
        rw, s_new = _rwkv(cols, rg, jnp.zeros((bp, ncols), F32), jnp.zeros((bp, heads_r, HEAD_DIM, HEAD_DIM), F32),
                          lp, nseq_total=bp, rows=chunk, tlen=chunk, nsub=4, t_valid=chunk, carry=True)
        df = _attn_prompt(q, kb, vt, dg, subw, lam_vecs, batch=bp, seq=seq, tq=tile, lam_init=lam_init)
        xp = _outproj(xp, rw, df, wo_l, final_norm_w, final=last, tm=2 * tile)
        outs["sp"].append(s_new)
        outs["hp"].append(cols.reshape(bp, seq, ncols)[:, -1, :])

        cols, rg, q, k, v, dg = _proj(xs, norm_w[l], w_l, bounds, tm=tile, prompt=False)
        prev = jnp.pad(state_shift[l][:, None, :], ((0, 0), (0, tpad - 1), (0, 0))).reshape(bs * tpad, ncols)
        rw, sst = _rwkv(cols, rg, prev, state_all, lp, nseq_total=bs, rows=chunk, tlen=tpad, nsub=1, t_valid=tdec,
                        carry=False, layer=l, depth=depth, stacked=() if l == 0 else (sst,))
        r3 = lambda t: t.reshape(bs, tpad, -1)
        df = _attn_sample(r3(q), r3(k), r3(v), r3(dg), subw, lam_vecs, cache_k, cache_v, page_table,
                          layer=l, lam_init=lam_init)
        xs = _outproj(xs, rw, df.reshape(bs * tpad, dwidth), wo_l, final_norm_w, final=last, tm=tile)
        outs["ks"].append(r3(k)[:, :tdec].reshape(bs, tdec, heads_d, PAIR))
        outs["vs"].append(r3(v)[:, :tdec].reshape(bs, tdec, heads_d, PAIR))
        outs["hs"].append(r3(cols)[:, tdec - 1, :])

    y_prompt = xp.reshape(bp, seq, d)
    y_sample = xs.reshape(bs, tpad, d)[:, :tdec]
    st = lambda n: jnp.stack(outs[n], axis=0)
    k_prompt = kst.reshape(depth, bp, seq, heads_d, PAIR)
    v_prompt = vst.reshape(depth, bp, seq, heads_d, PAIR)
    return (y_prompt, y_sample, k_prompt, v_prompt, st("sp"), st("hp"), st("ks"), st("vs"), sst, st("hs"))
```

```python
import functools
import math

import jax
import jax.numpy as jnp
from jax import lax
from jax.experimental import pallas as pl
from jax.experimental.pallas import tpu as pltpu

F32 = jnp.float32
BF16 = jnp.bfloat16

RMS_EPS = 1e-6
SUBLN_EPS = 1e-5
GN_EPS = 64e-5

LANES = 128
SUBLANES = 8
HEAD_DIM = 64
PAIR = 2 * HEAD_DIM
VMEM_LIMIT = 48 * 1024 * 1024
BF16_ROWS = 16
PAGE_SLOTS = 3
PROMPT_Q_SCALE = math.log2(math.e) / math.sqrt(HEAD_DIM)

_NT = (((1,), (1,)), ((), ()))
_TN = (((0,), (0,)), ((), ()))


def _dot(a, b):
    return jnp.dot(a, b, preferred_element_type=F32)


def _dot_nt(a, b):
    return lax.dot_general(a, b, _NT, preferred_element_type=F32)


def _dot_tn(a, b):
    return lax.dot_general(a, b, _TN, preferred_element_type=F32)


def _sigmoid(x):
    return 1.0 / (1.0 + jnp.exp(-x))


def _silu(x):
    return x * _sigmoid(x)


def _cat(xs, axis):
    return xs[0] if len(xs) == 1 else jnp.concatenate(xs, axis=axis)


def _params(sem):
    return pltpu.CompilerParams(dimension_semantics=sem, vmem_limit_bytes=VMEM_LIMIT)


def _store_token_head(dst, val, heads):
    rows = val.shape[0]
    for h in range(heads):
        dst[pl.ds(h, rows, stride=heads), :] = val[:, h * PAIR:(h + 1) * PAIR]


def _proj_kernel(x_ref, nw_ref, w_ref, *refs, bounds, prompt, layer):
    x = x_ref[...]
    h = x * lax.rsqrt(jnp.mean(x * x, axis=-1, keepdims=True) + RMS_EPS) * nw_ref[...]
    hb = h.astype(BF16)
    c0, c1, c2, c3, c4, c5 = bounds
    mm = lambda lo, hi: _dot(hb, w_ref[:, lo:hi])
    if prompt:
        cols_ref, rg_ref, q_ref, kb_ref, vt_ref, dg_ref, kst_ref, vst_ref = refs[-8:]
    else:
        cols_ref, rg_ref, q_ref, k_ref, v_ref, dg_ref = refs
    cols_ref[...] = mm(0, c0)
    rg_ref[...] = mm(c0, c1)
    q = mm(c1, c2)
    k = mm(c2, c3)
    v = mm(c3, c4)
    dg_ref[...] = mm(c4, c5)
    if not prompt:
        q_ref[...] = q
        k_ref[...] = k
        v_ref[...] = v
        return
    q_ref[...] = (q * PROMPT_Q_SCALE).astype(BF16)
    kb_ref[...] = k.astype(BF16)
    vt_ref[0] = v.T.astype(BF16)
    heads = (c3 - c2) // PAIR
    if layer == 0:
        for slot in range(kst_ref.shape[0]):
            _store_token_head(kst_ref.at[slot], k, heads)
            _store_token_head(vst_ref.at[slot], v, heads)
    else:
        _store_token_head(kst_ref, k, heads)
        _store_token_head(vst_ref, v, heads)


def _proj(x2d, norm_w, w_bf16, bounds, *, tm, prompt, seq=None, layer=0, depth=1, stacked=()):
    m, d = x2d.shape
    c0, c1, c2, c3, c4, c5 = bounds
    row = lambda i: (i, 0)
    const = lambda i: (0, 0)
    wide = lambda w, dt: (pl.BlockSpec((tm, w), row), jax.ShapeDtypeStruct((m, w), dt))
    in_specs = [pl.BlockSpec((tm, d), row), pl.BlockSpec((1, d), const), pl.BlockSpec(w_bf16.shape, const)]
    aliases = {}
    if prompt:
        nt = seq // tm
        heads = (c3 - c2) // PAIR
        outs = [wide(c0, F32), wide(c1 - c0, F32), wide(c2 - c1, BF16), wide(c3 - c2, BF16),
                (pl.BlockSpec((1, c4 - c3, tm), lambda i: (i // nt, 0, i % nt)),
                 jax.ShapeDtypeStruct((m // seq, c4 - c3, seq), BF16)),
                wide(c5 - c4, F32)]
        if layer == 0:
            st_spec = pl.BlockSpec((depth, tm * heads, PAIR), lambda i: (0, i, 0))
        else:
            st_spec = pl.BlockSpec((None, tm * heads, PAIR), lambda i: (layer, i, 0))
            in_specs += [pl.BlockSpec(memory_space=pl.ANY)] * 2
            aliases = {3: 6, 4: 7}
        outs += [(st_spec, jax.ShapeDtypeStruct((depth, m * heads, PAIR), F32))] * 2
    else:
        outs = [wide(c0, F32), wide(c1 - c0, F32), wide(c2 - c1, F32), wide(c3 - c2, F32), wide(c4 - c3, F32),
                wide(c5 - c4, F32)]
    return pl.pallas_call(
        functools.partial(_proj_kernel, bounds=bounds, prompt=prompt, layer=layer),
        grid=(m // tm,),
        in_specs=in_specs,
        out_specs=[o[0] for o in outs],
        out_shape=[o[1] for o in outs],
        input_output_aliases=aliases,
        compiler_params=_params(("arbitrary",)),
        name="proj_prompt" if prompt else "proj_sample",
    )(x2d, norm_w.reshape(1, d), w_bf16, *stacked)


def _head_sum(x):
    rows, width = x.shape
    lo = lax.broadcasted_iota(jnp.int32, (rows, PAIR), 1) < HEAD_DIM
    out = []
    for p in range(width // PAIR):
        xp = x[:, p * PAIR:(p + 1) * PAIR]
        s_lo = jnp.sum(jnp.where(lo, xp, 0.0), axis=-1, keepdims=True)
        s_hi = jnp.sum(jnp.where(lo, 0.0, xp), axis=-1, keepdims=True)
        out.append(jnp.where(lo, s_lo, s_hi))
    return jnp.concatenate(out, axis=-1)


def _split3(x):
    h0 = x.astype(BF16)
    r1 = x - h0.astype(F32)
    h1 = r1.astype(BF16)
    h2 = (r1 - h1.astype(F32)).astype(BF16)
    return h0, h1, h2


def _blockdiag(s_even, s_odd):
    zero = jnp.zeros((HEAD_DIM, HEAD_DIM), F32)
    return jnp.concatenate([jnp.concatenate([s_even, zero], axis=1), jnp.concatenate([zero, s_odd], axis=1)], axis=0)


def _rwkv_kernel(cols_ref, gate_ref, prev_ref, s0_ref, mu_ref, w0_ref, lora_ref, a0_ref, keyk_ref, keya_ref,
                 bonus_ref, lnw_ref, lnb_ref, *refs, rows, tlen, nsub, width, t_valid, carry, nslot):
    out_ref, sout_ref, s_scr, prev_scr = refs[-4:]
    tb = pl.program_id(1)
    nblk = pl.num_programs(1)
    npair = width // PAIR
    nseq = rows // tlen
    r2 = 2 * rows
    blk_rows = rows * nsub

    c = cols_ref[...]
    row_t = lax.broadcasted_iota(jnp.int32, c.shape, 0) % tlen if not carry else None
    if carry:
        @pl.when(tb == 0)
        def _():
            prev_scr[...] = prev_ref[0]
            for p in range(npair):
                s_scr[p] = _blockdiag(s0_ref[0, 2 * p], s0_ref[0, 2 * p + 1])

        row = lax.broadcasted_iota(jnp.int32, c.shape, 0)
        shifted = jnp.where(row == 0, prev_scr[...], pltpu.roll(c, 1, 0))
        prev_scr[...] = c[blk_rows - 1:blk_rows, :]
        state = {(0, p): s_scr[p] for p in range(npair)}
    else:
        shifted = jnp.where(row_t == 0, prev_ref[...], pltpu.roll(c, 1, 0))
        state = {(q, p): _blockdiag(s0_ref[q, 2 * p], s0_ref[q, 2 * p + 1])
                 for q in range(nsub * nseq) for p in range(npair)}

    xs = c + (shifted - c) * mu_ref[...]
    r = xs[:, 0:width]
    k = xs[:, width:2 * width]
    v = xs[:, 2 * width:3 * width]
    xwa = xs[:, 3 * width:3 * width + PAIR]
    lo_rank = lax.broadcasted_iota(jnp.int32, xwa.shape, 1) < HEAD_DIM
    xwa = jnp.where(lo_rank, jnp.tanh(xwa), xwa)
    lora = _dot(xwa.astype(BF16), lora_ref[...])
    z = -(w0_ref[...] + lora[:, 0:width])
    softplus = jnp.maximum(z, 0.0) + jnp.log(1.0 + jnp.exp(-jnp.abs(z)))
    lw = -jnp.exp(-softplus - 0.5)
    a = _sigmoid(a0_ref[...] + lora[:, width:2 * width])
    kk = k * keyk_ref[...]
    kk = kk / jnp.maximum(jnp.sqrt(_head_sum(kk * kk)), 1e-12)
    k2 = k * (1.0 + (a - 1.0) * keya_ref[...])
    av = -kk
    bv = kk * a
    if t_valid < tlen:
        live = lax.broadcasted_iota(jnp.int32, lw.shape, 0) % tlen < t_valid
        lw = jnp.where(live, lw, 0.0)
        av = jnp.where(live, av, 0.0)
        bv = jnp.where(live, bv, 0.0)
        k2s = jnp.where(live, k2, 0.0)
        vs = jnp.where(live, v, 0.0)
    else:
        k2s, vs = k2, v

    ri = lax.broadcasted_iota(jnp.int32, (r2, r2), 0)
    ci = lax.broadcasted_iota(jnp.int32, (r2, r2), 1)
    same = (ri // tlen) == (ci // tlen)
    strict = same & (ci < ri)
    incl = same & (ci <= ri)
    eye = (ri == ci).astype(F32)
    cum_lower = incl[0:rows, 0:rows]
    cum = jnp.concatenate([cum_lower, same[0:rows, 0:rows]], axis=0).astype(F32).astype(BF16)
    lo = lax.broadcasted_iota(jnp.int32, (rows, PAIR), 1) < HEAD_DIM
    nsteps = max(1, (tlen - 1).bit_length())

    def stack(xp):
        return jnp.concatenate([jnp.where(lo, xp, 0.0), jnp.where(lo, 0.0, xp)], axis=0)

    def seq_rows(x, q):
        if nseq == 1:
            return x
        return jnp.concatenate([x[q * tlen:(q + 1) * tlen], x[rows + q * tlen:rows + (q + 1) * tlen]], axis=0)

    def unseq(parts):
        if nseq == 1:
            return parts[0]
        return jnp.concatenate([x[0:tlen] for x in parts] + [x[tlen:2 * tlen] for x in parts], axis=0)

    probs = []
    for sub in range(nsub):
        sl = slice(sub * rows, (sub + 1) * rows)
        lwc = lw[sl]
        g0, g1, g2 = _split3(lwc)
        gg = _dot(cum, g0) + _dot(cum, g1) + _dot(cum, g2)
        g = gg[0:rows]
        g_end = gg[rows:r2]
        eg = jnp.exp(g)
        eng = jnp.exp(-g)
        egc = jnp.exp(g_end - g)
        at = av[sl] * jnp.exp(g - lwc)
        rt = r[sl] * eg
        bt = bv[sl] * eng
        kt = k2s[sl] * eng
        bh = bv[sl] * egc
        kh = k2s[sl] * egc
        e_end = jnp.exp(g_end)
        vc = vs[sl]
        for p in range(npair):
            ps = slice(p * PAIR, (p + 1) * PAIR)
            probs.append(dict(
                sub=sub, p=p,
                la=stack(at[:, ps]).astype(BF16), lr=stack(rt[:, ps]), lb=stack(bt[:, ps]).astype(BF16),
                lk=stack(kt[:, ps]).astype(BF16), lbh=stack(bh[:, ps]), lkh=stack(kh[:, ps]), lv=stack(vc[:, ps]),
                e_end=e_end[:, ps]))
    for pr in probs:
        pr["lrb"] = pr["lr"].astype(BF16)
        pr["lvb"] = pr["lv"].astype(BF16)
        sc = _dot_nt(jnp.concatenate([pr["la"], pr["lrb"]], axis=0),
                     jnp.concatenate([pr["lb"], pr["lk"]], axis=0))
        pr["pw"] = jnp.where(strict, sc[0:r2, 0:r2], 0.0)
        pr["dak"] = jnp.where(strict, sc[0:r2, r2:2 * r2], 0.0)
        pr["drb"] = jnp.where(incl, sc[r2:2 * r2, 0:r2], 0.0).astype(BF16)
        pr["drk"] = jnp.where(incl, sc[r2:2 * r2, r2:2 * r2], 0.0)
        pr["inv"] = eye + pr["pw"]
    for step in range(nsteps):
        for pr in probs:
            pb = pr["pw"].astype(BF16)
            if step == 0:
                if nsteps > 1:
                    pr["pw"] = _dot(pb, pb)
            elif step < nsteps - 1:
                px = _dot(pb, jnp.concatenate([pb, pr["inv"].astype(BF16)], axis=1))
                pr["pw"] = px[:, 0:r2]
                pr["inv"] = pr["inv"] + px[:, r2:2 * r2]
            else:
                pr["inv"] = pr["inv"] + _dot(pb, pr["inv"].astype(BF16))
    for pr in probs:
        pr["dv"] = _dot(jnp.concatenate([pr["dak"], pr["drk"]], axis=0).astype(BF16), pr["lvb"])
    for pr in probs:
        tw = _dot(pr["inv"].astype(BF16), jnp.concatenate([pr["la"], pr["dv"][0:r2].astype(BF16)], axis=1))
        pr["lw"] = tw[:, 0:PAIR]
        pr["lu0"] = tw[:, PAIR:2 * PAIR]

    y_chunks = []
    for sub in range(nsub):
        sub_probs = [pr for pr in probs if pr["sub"] == sub]
        for pr in sub_probs:
            pr["ws"] = []
            for q in range(nseq):
                key = (0 if carry else sub * nseq + q, pr["p"])
                lhs = jnp.concatenate([seq_rows(pr["lw"], q), seq_rows(pr["lr"], q)], axis=0).astype(BF16)
                pr["ws"].append(_dot_nt(lhs, state[key].astype(BF16)))
        for pr in sub_probs:
            lu = unseq([w[0:2 * tlen] for w in pr["ws"]]) + pr["lu0"]
            pr["lu"] = lu
            y2 = unseq([w[2 * tlen:4 * tlen] for w in pr["ws"]]) + _dot(pr["drb"], lu.astype(BF16)) + pr["dv"][r2:2 * r2]
            pr["y"] = y2[0:rows] + y2[rows:r2]
        for pr in sub_probs:
            for q in range(nseq):
                key = (0 if carry else sub * nseq + q, pr["p"])
                lhs = jnp.concatenate([seq_rows(pr["lu"], q), seq_rows(pr["lv"], q)], axis=0).astype(BF16)
                rhs = jnp.concatenate([seq_rows(pr["lbh"], q), seq_rows(pr["lkh"], q)], axis=0).astype(BF16)
                decay = pr["e_end"][q * tlen:q * tlen + 1, :]
                state[key] = state[key] * decay + _dot_tn(lhs, rhs)
        y_chunks.append(_cat([pr["y"] for pr in sub_probs], axis=-1))
    y = _cat(y_chunks, axis=0)

    inv_n = 1.0 / HEAD_DIM
    mean = _head_sum(y) * inv_n
    d = y - mean
    var = _head_sum(d * d) * inv_n
    yn = d * lax.rsqrt(var + GN_EPS) * lnw_ref[...] + lnb_ref[...]
    bonus = _head_sum(r * k2 * bonus_ref[...]) * v
    out_ref[...] = (yn + bonus) * _silu(gate_ref[...])

    views = [sout_ref.at[s] for s in range(sout_ref.shape[0])] if nslot else [sout_ref]

    def write_state(q, src):
        for p in range(npair):
            sp = src[(q, p)]
            for dst in views:
                dst[q, 2 * p] = sp[0:HEAD_DIM, 0:HEAD_DIM]
                dst[q, 2 * p + 1] = sp[HEAD_DIM:PAIR, HEAD_DIM:PAIR]

    if carry:
        for p in range(npair):
            s_scr[p] = state[(0, p)]

        @pl.when(tb == nblk - 1)
        def _():
            write_state(0, state)
    else:
        for q in range(nsub * nseq):
            write_state(q, state)


def _rwkv(cols, gate, prev, s0, lp, *, nseq_total, rows, tlen, nsub, t_valid, carry, layer=0, depth=1, stacked=()):
    m, ncols = cols.shape
    width = gate.shape[1]
    blk_rows = rows * nsub
    heads = width // HEAD_DIM
    const = lambda b, t: (0, 0)
    vec = lambda n: pl.BlockSpec((1, n), const)
    hd = (heads, HEAD_DIM, HEAD_DIM)
    aliases = {}
    extra_specs = []
    nslot = 0
    if carry:
        nblk = m // nseq_total // blk_rows
        grid = (nseq_total, nblk)
        blk = lambda b, t: (b * nblk + t, 0)
        prev_arr = prev.reshape(nseq_total, 1, ncols)
        prev_spec = pl.BlockSpec((1, 1, ncols), lambda b, t: (b, 0, 0))
        s0_spec = sout_spec = pl.BlockSpec((1,) + hd, lambda b, t: (b, 0, 0, 0))
        sout_shape = (nseq_total,) + hd
    else:
        nstep = m // blk_rows
        grid = (nstep, 1)
        blk = lambda b, t: (b, 0)
        prev_arr = prev
        prev_spec = pl.BlockSpec((blk_rows, ncols), blk)
        sblk = blk_rows // tlen
        s0_spec = pl.BlockSpec((sblk,) + hd, lambda b, t: (b, 0, 0, 0))
        sout_shape = (depth, nseq_total) + hd
        if layer == 0:
            nslot = depth
            sout_spec = pl.BlockSpec((depth, sblk) + hd, lambda b, t: (0, b, 0, 0, 0))
        else:
            sout_spec = pl.BlockSpec((None, sblk) + hd, lambda b, t: (layer, b, 0, 0, 0))
            extra_specs = [pl.BlockSpec(memory_space=pl.ANY)]
            aliases = {13: 1}
    return pl.pallas_call(
        functools.partial(_rwkv_kernel, rows=rows, tlen=tlen, nsub=nsub, width=width, t_valid=t_valid, carry=carry,
                          nslot=nslot),
        grid=grid,
        in_specs=[pl.BlockSpec((blk_rows, ncols), blk), pl.BlockSpec((blk_rows, width), blk), prev_spec, s0_spec,
                  vec(ncols), vec(width), pl.BlockSpec((PAIR, 2 * width), const), vec(width), vec(width),
                  vec(width), vec(width), vec(width), vec(width)] + extra_specs,
        out_specs=[pl.BlockSpec((blk_rows, width), blk), sout_spec],
        out_shape=[jax.ShapeDtypeStruct((m, width), F32), jax.ShapeDtypeStruct(sout_shape, F32)],
        input_output_aliases=aliases,
        scratch_shapes=[pltpu.VMEM((width // PAIR, PAIR, PAIR), F32), pltpu.VMEM((1, ncols), F32)],
        compiler_params=_params(("arbitrary", "arbitrary")),
        name="rwkv_prompt" if carry else "rwkv_sample",
    )(cols, gate, prev_arr, s0, lp["mu"], lp["w0"], lp["lora"], lp["a0"], lp["key_k"],
      lp["key_a"], lp["bonus"], lp["lnx_w"], lp["lnx_b"], *stacked)


def _lambda(lam_ref, lam_init):
    lam = lam_ref[...]
    e1 = jnp.exp(jnp.sum(lam[0:1] * lam[1:2], axis=-1, keepdims=True))
    e2 = jnp.exp(jnp.sum(lam[2:3] * lam[3:4], axis=-1, keepdims=True))
    return e1 - e2 + lam_init


def _subln_gate(o, subw, gate, lam_init):
    o = o * lax.rsqrt(jnp.mean(o * o, axis=-1, keepdims=True) + SUBLN_EPS) * subw
    return o * (1.0 - lam_init) * _silu(gate)


def _attn_prompt_kernel(q_ref, k_ref, vt_ref, dg_ref, subw_ref, lam_ref, out_ref, vx_scr, *, tq, lam_init):
    seq = q_ref.shape[0]
    lo = lax.broadcasted_iota(jnp.int32, (tq, PAIR), 1) < HEAD_DIM
    ki = lax.broadcasted_iota(jnp.int32, (tq, 2 * tq), 0)
    qi = lax.broadcasted_iota(jnp.int32, (tq, 2 * tq), 1)
    causal = ki <= jnp.where(qi >= tq, qi - tq, qi)
    lam = _lambda(lam_ref, lam_init)
    vx_scr[0:PAIR, :] = vt_ref[...]
    vx_scr[PAIR:PAIR + BF16_ROWS, :] = jnp.ones((BF16_ROWS, seq), BF16)
    def scores(i):
        past, here = i * tq, (i + 1) * tq
        q = q_ref[past:here, :]
        zero = jnp.zeros_like(q)
        qs = jnp.concatenate([jnp.where(lo, q, zero), jnp.where(lo, zero, q)], axis=0)
        s_diag = jnp.where(causal, _dot_nt(k_ref[past:here, :], qs), -jnp.inf)
        s_past = _dot_nt(k_ref[0:past, :], qs) if i > 0 else None
        return s_diag, s_past

    nq = seq // tq
    ahead = scores(0)
    for i in range(nq):
        past, here = i * tq, (i + 1) * tq
        s_diag, s_past = ahead
        if i + 1 < nq:
            ahead = scores(i + 1)
        m = jnp.max(s_diag, axis=0, keepdims=True)
        if i > 0:
            m = jnp.maximum(m, jnp.max(s_past, axis=0, keepdims=True))
        acc = _dot(vx_scr[:, past:here], jnp.exp2(s_diag - m).astype(BF16))
        if i > 0:
            acc = acc + _dot(vx_scr[:, 0:past], jnp.exp2(s_past - m).astype(BF16))
        ot = acc[0:PAIR] / acc[PAIR:PAIR + 1]
        ot = ot[:, 0:tq] - lam * ot[:, tq:2 * tq]
        out_ref[past:here, :] = _subln_gate(ot.T, subw_ref[...], dg_ref[past:here, :], lam_init)


def _attn_prompt(q, kb, vt, dg, subw, lam_vecs, *, batch, seq, tq, lam_init):
    m, width = q.shape
    blk = pl.BlockSpec((seq, PAIR), lambda b, h: (b, h))
    const = lambda b, h: (0, 0)
    return pl.pallas_call(
        functools.partial(_attn_prompt_kernel, tq=tq, lam_init=lam_init),
        grid=(batch, width // PAIR),
        in_specs=[blk, blk, pl.BlockSpec((None, PAIR, seq), lambda b, h: (b, h, 0)), blk,
                  pl.BlockSpec((1, PAIR), const), pl.BlockSpec((4, HEAD_DIM), const)],
        out_specs=blk,
        out_shape=jax.ShapeDtypeStruct((m, width), F32),
        scratch_shapes=[pltpu.VMEM((PAIR + BF16_ROWS, seq), BF16)],
        compiler_params=_params(("arbitrary", "arbitrary")),
        name="attn_prompt",
    )(q, kb, vt, dg, subw, lam_vecs)


def _attn_sample_kernel(pt_ref, q_ref, kn_ref, vn_ref, dg_ref, subw_ref, lam_ref, ck_ref, cv_ref, out_ref,
                        kbuf, vbuf, sem, *, npages, tpad, heads, lam_init, nseq):
    b = pl.program_id(0)
    prow = kbuf.shape[2]

    def fetch(seq, slot):
        out = []
        for j in range(npages):
            row = pl.multiple_of(pt_ref[seq * npages + j] * prow, prow)
            out.append(pltpu.make_async_copy(ck_ref.at[pl.ds(row, prow), :], kbuf.at[slot, j], sem.at[slot, 0]))
            out.append(pltpu.make_async_copy(cv_ref.at[pl.ds(row, prow), :], vbuf.at[slot, j], sem.at[slot, 1]))
        return out

    def wait(slot):
        for j in range(npages):
            pltpu.make_async_copy(ck_ref.at[pl.ds(0, prow), :], kbuf.at[slot, j], sem.at[slot, 0]).wait()
            pltpu.make_async_copy(cv_ref.at[pl.ds(0, prow), :], vbuf.at[slot, j], sem.at[slot, 1]).wait()

    @pl.when(b == 0)
    def _():
        for s in range(PAGE_SLOTS - 1):
            for c in fetch(min(s, nseq - 1), s):
                c.start()

    slot = b % PAGE_SLOTS
    wait(slot)
    nrow = 2 * heads * tpad
    half = nrow // 2
    q = q_ref[0] * (1.0 / math.sqrt(HEAD_DIM))
    lo = lax.broadcasted_iota(jnp.int32, (tpad, PAIR), 1) < HEAD_DIM
    pieces = []
    for c in range(2):
        for h in range(heads):
            qh = q[:, h * PAIR:(h + 1) * PAIR]
            pieces.append(jnp.where(lo, qh, 0.0) if c == 0 else jnp.where(lo, 0.0, qh))
    qst = jnp.concatenate(pieces, axis=0).astype(BF16)

    def two(buf, j):
        return jnp.concatenate([buf[slot, j].astype(BF16), buf[slot, j + 1].astype(BF16)], axis=0)

    rhead = (lax.broadcasted_iota(jnp.int32, (nrow, 2 * prow), 0) // tpad) % heads
    chead = lax.broadcasted_iota(jnp.int32, (nrow, 2 * prow), 1) % heads
    own = rhead == chead
    scores = [jnp.where(own, _dot_nt(qst, two(kbuf, j)), -jnp.inf) for j in range(0, npages, 2)]

    def new_rows(ref):
        x = ref[0]
        xs = [x[:, h * PAIR:(h + 1) * PAIR] for h in range(heads)]
        xs.append(jnp.zeros((LANES - heads * tpad, PAIR), F32))
        return jnp.concatenate(xs, axis=0).astype(BF16)

    s_new = _dot_nt(qst, new_rows(kn_ref))
    rn = lax.broadcasted_iota(jnp.int32, (nrow, LANES), 0)
    cn = lax.broadcasted_iota(jnp.int32, (nrow, LANES), 1)
    visible = ((rn // tpad) % heads == cn // tpad) & (cn % tpad <= rn % tpad) & (cn < heads * tpad)
    s_new = jnp.where(visible, s_new, -jnp.inf)
    m = jnp.max(s_new, axis=-1, keepdims=True)
    for s in scores:
        m = jnp.maximum(m, jnp.max(s, axis=-1, keepdims=True))
    p_new = jnp.exp(s_new - m)
    l = jnp.sum(p_new, axis=-1, keepdims=True)
    probs = []
    for s in scores:
        p = jnp.exp(s - m)
        l = l + jnp.sum(p, axis=-1, keepdims=True)
        probs.append(p)
    inv_l = 1.0 / l
    w1 = inv_l[0:half]
    w2 = _lambda(lam_ref, lam_init) * inv_l[half:nrow]

    def mix(p):
        return (p[0:half] * w1 - p[half:nrow] * w2).astype(BF16)

    o = _dot(mix(p_new), new_rows(vn_ref))
    for idx, j in enumerate(range(0, npages, 2)):
        o = o + _dot(mix(probs[idx]), two(vbuf, j))
    dg = dg_ref[0]
    outs = [_subln_gate(o[h * tpad:(h + 1) * tpad], subw_ref[...], dg[:, h * PAIR:(h + 1) * PAIR], lam_init)
            for h in range(heads)]
    out_ref[0] = jnp.concatenate(outs, axis=-1)

    ahead = jnp.minimum(b + PAGE_SLOTS - 1, nseq - 1)
    for c in fetch(ahead, (b + PAGE_SLOTS - 1) % PAGE_SLOTS):
        c.start()

    @pl.when(b == nseq - 1)
    def _():
        for s in range(1, PAGE_SLOTS):
            wait((nseq - 1 + s) % PAGE_SLOTS)


def _attn_sample(q, kn, vn, dg, subw, lam_vecs, cache_k, cache_v, page_table, *, layer, lam_init):
    batch, tpad, width = q.shape
    npages = page_table.shape[1]
    n_phys, page, heads = cache_k.shape[1], cache_k.shape[2], cache_k.shape[3]
    prow = page * heads
    ck = cache_k.reshape(-1, PAIR)
    cv = cache_v.reshape(-1, PAIR)
    pt = page_table.reshape(-1).astype(jnp.int32) + layer * n_phys
    tok = pl.BlockSpec((1, tpad, width), lambda b, pt: (b, 0, 0))
    const = lambda b, pt: (0, 0)

    grid_spec = pltpu.PrefetchScalarGridSpec(
        num_scalar_prefetch=1,
        grid=(batch,),
        in_specs=[tok, tok, tok, tok, pl.BlockSpec((1, PAIR), const), pl.BlockSpec((4, HEAD_DIM), const),
                  pl.BlockSpec(memory_space=pl.ANY), pl.BlockSpec(memory_space=pl.ANY)],
        out_specs=tok,
        scratch_shapes=[pltpu.VMEM((PAGE_SLOTS, npages, prow, PAIR), F32),
                        pltpu.VMEM((PAGE_SLOTS, npages, prow, PAIR), F32),
                        pltpu.SemaphoreType.DMA((PAGE_SLOTS, 2))],
    )
    return pl.pallas_call(
        functools.partial(_attn_sample_kernel, npages=npages, tpad=tpad, heads=heads, lam_init=lam_init,
                          nseq=batch),
        grid_spec=grid_spec,
        out_shape=jax.ShapeDtypeStruct((batch, tpad, width), F32),
        compiler_params=_params(("arbitrary",)),
        name="attn_sample",
    )(pt, q, kn, vn, dg, subw, lam_vecs, ck, cv)


def _outproj_kernel(x_ref, r_ref, d_ref, w_ref, fw_ref, o_ref, *, final):
    half = r_ref.shape[1]
    y = x_ref[...] + _dot(r_ref[...].astype(BF16), w_ref[0:half, :]) + _dot(d_ref[...].astype(BF16), w_ref[half:, :])
    if final:
        y = y * lax.rsqrt(jnp.mean(y * y, axis=-1, keepdims=True) + RMS_EPS) * fw_ref[...]
    o_ref[...] = y


def _outproj(x2d, rw, df, w_bf16, final_w, *, final, tm):
    m, d = x2d.shape
    half = rw.shape[1]
    row = lambda i: (i, 0)
    const = lambda i: (0, 0)
    return pl.pallas_call(
        functools.partial(_outproj_kernel, final=final),
        grid=(m // tm,),
        in_specs=[pl.BlockSpec((tm, d), row), pl.BlockSpec((tm, half), row), pl.BlockSpec((tm, half), row),
                  pl.BlockSpec(w_bf16.shape, const), pl.BlockSpec((1, d), const)],
        out_specs=pl.BlockSpec((tm, d), row),
        out_shape=jax.ShapeDtypeStruct((m, d), F32),
        compiler_params=_params(("arbitrary",)),
        name="outproj",
    )(x2d, rw, df, w_bf16, final_w.reshape(1, d))


def kernel(x_prompt, x_sample, cache_k, cache_v, page_table, state_rwkv, state_shift, norm_w, w_in, shift_mu, decay_w0, decay_lora_b, iclr_a0, iclr_lora_b, key_k, key_a, bonus_r_k, lnx_w, lnx_b, lam_q1, lam_k1, lam_q2, lam_k2, subln_w, w_out, final_norm_w):
    depth = w_in.shape[0]
    bp, seq, d = x_prompt.shape
    bs, tdec, _ = x_sample.shape
    width = decay_w0.shape[-1]
    ncols = shift_mu.shape[-1]
    rank_w, rank_a = decay_lora_b.shape[1], iclr_lora_b.shape[1]
    heads_r = width // HEAD_DIM
    heads_d = cache_k.shape[3]
    dwidth = heads_d * cache_v.shape[4]
    dqk = heads_d * cache_k.shape[4]
    assert rank_w == HEAD_DIM and rank_a == HEAD_DIM and ncols == 3 * width + PAIR
    assert cache_k.shape[4] == PAIR and cache_v.shape[4] == PAIR
    c0 = ncols
    c1 = c0 + width
    c2 = c1 + dqk
    c3 = c2 + dqk
    c4 = c3 + dwidth
    c5 = c4 + dwidth
    bounds = (c0, c1, c2, c3, c4, c5)
    assert w_in.shape[2] == c5

    tpad = SUBLANES
    chunk = 64
    tile = 256
    assert tdec <= tpad
    xp = x_prompt.reshape(bp * seq, d)
    xs = jnp.pad(x_sample, ((0, 0), (0, tpad - tdec), (0, 0))).reshape(bs * tpad, d)

    kst = vst = sst = None
    outs = {n: [] for n in ("sp", "hp", "ks", "vs", "hs")}
    for l in range(depth):
        lam_init = 0.8 - 0.6 * math.exp(-0.3 * l)
        w_l = w_in[l].astype(BF16)
        wo_l = w_out[l].astype(BF16)
        zeros = jnp.zeros((rank_w, width), F32)
        lora = jnp.concatenate([jnp.concatenate([decay_lora_b[l], zeros], axis=1),
                                jnp.concatenate([zeros, iclr_lora_b[l]], axis=1)], axis=0).astype(BF16)
        lp = dict(mu=shift_mu[l].reshape(1, ncols), w0=decay_w0[l].reshape(1, width), lora=lora,
                  a0=iclr_a0[l].reshape(1, width), key_k=key_k[l].reshape(1, width), key_a=key_a[l].reshape(1, width),
                  bonus=bonus_r_k[l].reshape(1, width), lnx_w=lnx_w[l].reshape(1, width),
                  lnx_b=lnx_b[l].reshape(1, width))
        lam_vecs = jnp.stack([lam_q1[l], lam_k1[l], lam_q2[l], lam_k2[l]], axis=0)
        subw = subln_w[l].reshape(1, PAIR)
        last = l == depth - 1

        cols, rg, q, kb, vt, dg, kst, vst = _proj(xp, norm_w[l], w_l, bounds, tm=tile, prompt=True, seq=seq,
                                                  layer=l, depth=depth, stacked=() if l == 0 else (kst, vst))
        rw, s_new = _rwkv(cols, rg, jnp.zeros((bp, ncols), F32), jnp.zeros((bp, heads_r, HEAD_DIM, HEAD_DIM), F32),
                          lp, nseq_total=bp, rows=chunk, tlen=chunk, nsub=4, t_valid=chunk, carry=True)
        df = _attn_prompt(q, kb, vt, dg, subw, lam_vecs, batch=bp, seq=seq, tq=tile, lam_init=lam_init)
        xp = _outproj(xp, rw, df, wo_l, final_norm_w, final=last, tm=2 * tile)
        outs["sp"].append(s_new)
        outs["hp"].append(cols.reshape(bp, seq, ncols)[:, -1, :])

        cols, rg, q, k, v, dg = _proj(xs, norm_w[l], w_l, bounds, tm=tile, prompt=False)
        prev = jnp.pad(state_shift[l][:, None, :], ((0, 0), (0, tpad - 1), (0, 0))).reshape(bs * tpad, ncols)
        rw, sst = _rwkv(cols, rg, prev, state_rwkv[l], lp, nseq_total=bs, rows=chunk, tlen=tpad, nsub=1, t_valid=tdec,
                        carry=False, layer=l, depth=depth, stacked=() if l == 0 else (sst,))
        r3 = lambda t: t.reshape(bs, tpad, -1)
        df = _attn_sample(r3(q), r3(k), r3(v), r3(dg), subw, lam_vecs, cache_k, cache_v, page_table,
                          layer=l, lam_init=lam_init)
        xs = _outproj(xs, rw, df.reshape(bs * tpad, dwidth), wo_l, final_norm_w, final=last, tm=tile)
        outs["ks"].append(r3(k)[:, :tdec].reshape(bs, tdec, heads_d, PAIR))
        outs["vs"].append(r3(v)[:, :tdec].reshape(bs, tdec, heads_d, PAIR))
        outs["hs"].append(r3(cols)[:, tdec - 1, :])

    y_prompt = xp.reshape(bp, seq, d)
    y_sample = xs.reshape(bs, tpad, d)[:, :tdec]
    st = lambda n: jnp.stack(outs[n], axis=0)
    k_prompt = kst.reshape(depth, bp, seq, heads_d, PAIR)
    v_prompt = vst.reshape(depth, bp, seq, heads_d, PAIR)
    return (y_prompt, y_sample, k_prompt, v_prompt, st("sp"), st("hp"), st("ks"), st("vs"), sst, st("hs"))
```

```python
import functools
import math

import jax
import jax.numpy as jnp
from jax import lax
from jax.experimental import pallas as pl
from jax.experimental.pallas import tpu as pltpu

F32 = jnp.float32
BF16 = jnp.bfloat16

RMS_EPS = 1e-6
SUBLN_EPS = 1e-5
GN_EPS = 64e-5

LANES = 128
SUBLANES = 8
HEAD_DIM = 64
PAIR = 2 * HEAD_DIM
VMEM_LIMIT = 48 * 1024 * 1024
BF16_ROWS = 16
PAGE_SLOTS = 3
PROMPT_Q_SCALE = math.log2(math.e) / math.sqrt(HEAD_DIM)

_NT = (((1,), (1,)), ((), ()))
_TN = (((0,), (0,)), ((), ()))


def _dot(a, b):
    return jnp.dot(a, b, preferred_element_type=F32)


def _dot_nt(a, b):
    return lax.dot_general(a, b, _NT, preferred_element_type=F32)


def _dot_tn(a, b):
    return lax.dot_general(a, b, _TN, preferred_element_type=F32)


def _sigmoid(x):
    return 1.0 / (1.0 + jnp.exp(-x))


def _silu(x):
    return x * _sigmoid(x)


def _cat(xs, axis):
    return xs[0] if len(xs) == 1 else jnp.concatenate(xs, axis=axis)


def _params(sem):
    return pltpu.CompilerParams(dimension_semantics=sem, vmem_limit_bytes=VMEM_LIMIT)


def _store_token_head(dst, val, heads):
    rows = val.shape[0]
    for h in range(heads):
        dst[pl.ds(h, rows, stride=heads), :] = val[:, h * PAIR:(h + 1) * PAIR]


def _proj_kernel(x_ref, nw_ref, w_ref, *refs, bounds, prompt, layer):
    x = x_ref[...]
    h = x * lax.rsqrt(jnp.mean(x * x, axis=-1, keepdims=True) + RMS_EPS) * nw_ref[...]
    hb = h.astype(BF16)
    c0, c1, c2, c3, c4, c5 = bounds
    mm = lambda lo, hi: _dot(hb, w_ref[:, lo:hi])
    if prompt:
        cols_ref, rg_ref, q_ref, kb_ref, vt_ref, dg_ref, kst_ref, vst_ref = refs[-8:]
    else:
        cols_ref, rg_ref, q_ref, k_ref, v_ref, dg_ref = refs
    cols_ref[...] = mm(0, c0)
    rg_ref[...] = mm(c0, c1)
    q = mm(c1, c2)
    k = mm(c2, c3)
    v = mm(c3, c4)
    dg_ref[...] = mm(c4, c5)
    if not prompt:
        q_ref[...] = q
        k_ref[...] = k
        v_ref[...] = v
        return
    q_ref[...] = (q * PROMPT_Q_SCALE).astype(BF16)
    kb_ref[...] = k.astype(BF16)
    vt_ref[0] = v.T.astype(BF16)
    heads = (c3 - c2) // PAIR
    if layer == 0:
        for slot in range(kst_ref.shape[0]):
            _store_token_head(kst_ref.at[slot], k, heads)
            _store_token_head(vst_ref.at[slot], v, heads)
    else:
        _store_token_head(kst_ref, k, heads)
        _store_token_head(vst_ref, v, heads)


def _proj(x2d, norm_w, w_bf16, bounds, *, tm, prompt, seq=None, layer=0, depth=1, stacked=()):
    m, d = x2d.shape
    c0, c1, c2, c3, c4, c5 = bounds
    row = lambda i: (i, 0)
    const = lambda i: (0, 0)
    wide = lambda w, dt: (pl.BlockSpec((tm, w), row), jax.ShapeDtypeStruct((m, w), dt))
    in_specs = [pl.BlockSpec((tm, d), row), pl.BlockSpec((1, d), const),
                pl.BlockSpec(w_bf16.shape, const, pipeline_mode=pl.Buffered(1))]
    aliases = {}
    if prompt:
        nt = seq // tm
        heads = (c3 - c2) // PAIR
        outs = [wide(c0, F32), wide(c1 - c0, F32), wide(c2 - c1, BF16), wide(c3 - c2, BF16),
                (pl.BlockSpec((1, c4 - c3, tm), lambda i: (i // nt, 0, i % nt)),
                 jax.ShapeDtypeStruct((m // seq, c4 - c3, seq), BF16)),
                wide(c5 - c4, F32)]
        if layer == 0:
            st_spec = pl.BlockSpec((depth, tm * heads, PAIR), lambda i: (0, i, 0))
        else:
            st_spec = pl.BlockSpec((None, tm * heads, PAIR), lambda i: (layer, i, 0))
            in_specs += [pl.BlockSpec(memory_space=pl.ANY)] * 2
            aliases = {3: 6, 4: 7}
        outs += [(st_spec, jax.ShapeDtypeStruct((depth, m * heads, PAIR), F32))] * 2
    else:
        outs = [wide(c0, F32), wide(c1 - c0, F32), wide(c2 - c1, F32), wide(c3 - c2, F32), wide(c4 - c3, F32),
                wide(c5 - c4, F32)]
    return pl.pallas_call(
        functools.partial(_proj_kernel, bounds=bounds, prompt=prompt, layer=layer),
        grid=(m // tm,),
        in_specs=in_specs,
        out_specs=[o[0] for o in outs],
        out_shape=[o[1] for o in outs],
        input_output_aliases=aliases,
        compiler_params=_params(("arbitrary",)),
        name="proj_prompt" if prompt else "proj_sample",
    )(x2d, norm_w.reshape(1, d), w_bf16, *stacked)


def _head_sum(x):
    rows, width = x.shape
    lo = lax.broadcasted_iota(jnp.int32, (rows, PAIR), 1) < HEAD_DIM
    out = []
    for p in range(width // PAIR):
        xp = x[:, p * PAIR:(p + 1) * PAIR]
        s_lo = jnp.sum(jnp.where(lo, xp, 0.0), axis=-1, keepdims=True)
        s_hi = jnp.sum(jnp.where(lo, 0.0, xp), axis=-1, keepdims=True)
        out.append(jnp.where(lo, s_lo, s_hi))
    return jnp.concatenate(out, axis=-1)


def _split3(x):
    h0 = x.astype(BF16)
    r1 = x - h0.astype(F32)
    h1 = r1.astype(BF16)
    h2 = (r1 - h1.astype(F32)).astype(BF16)
    return h0, h1, h2


N_RWKV_IN = 13
N_ATTN_IN = 8


def _blockdiag(s_even, s_odd):
    zero = jnp.zeros((HEAD_DIM, HEAD_DIM), F32)
    return jnp.concatenate([jnp.concatenate([s_even, zero], axis=1), jnp.concatenate([zero, s_odd], axis=1)], axis=0)


def _rwkv_kernel(cols_ref, gate_ref, prev_ref, s0_ref, mu_ref, w0_ref, lora_ref, a0_ref, keyk_ref, keya_ref,
                 bonus_ref, lnw_ref, lnb_ref, *refs, rows, tlen, nsub, width, t_valid, carry, nslot):
    out_ref, sout_ref, s_scr, prev_scr = refs[-4:]
    tb = pl.program_id(1)
    nblk = pl.num_programs(1)
    npair = width // PAIR
    nseq = rows // tlen
    r2 = 2 * rows
    blk_rows = rows * nsub

    c = cols_ref[...]
    row_t = lax.broadcasted_iota(jnp.int32, c.shape, 0) % tlen if not carry else None
    if carry:
        @pl.when(tb == 0)
        def _():
            prev_scr[...] = prev_ref[0]
            for p in range(npair):
                s_scr[p] = _blockdiag(s0_ref[0, 2 * p], s0_ref[0, 2 * p + 1])

        row = lax.broadcasted_iota(jnp.int32, c.shape, 0)
        shifted = jnp.where(row == 0, prev_scr[...], pltpu.roll(c, 1, 0))
        prev_scr[...] = c[blk_rows - 1:blk_rows, :]
        state = {(0, p): s_scr[p] for p in range(npair)}
    else:
        shifted = jnp.where(row_t == 0, prev_ref[...], pltpu.roll(c, 1, 0))
        state = {(q, p): _blockdiag(s0_ref[q, 2 * p], s0_ref[q, 2 * p + 1])
                 for q in range(nsub * nseq) for p in range(npair)}

    xs = c + (shifted - c) * mu_ref[...]
    r = xs[:, 0:width]
    k = xs[:, width:2 * width]
    v = xs[:, 2 * width:3 * width]
    xwa = xs[:, 3 * width:3 * width + PAIR]
    lo_rank = lax.broadcasted_iota(jnp.int32, xwa.shape, 1) < HEAD_DIM
    xwa = jnp.where(lo_rank, jnp.tanh(xwa), xwa)
    lora = _dot(xwa.astype(BF16), lora_ref[...])
    z = -(w0_ref[...] + lora[:, 0:width])
    softplus = jnp.maximum(z, 0.0) + jnp.log(1.0 + jnp.exp(-jnp.abs(z)))
    lw = -jnp.exp(-softplus - 0.5)
    a = _sigmoid(a0_ref[...] + lora[:, width:2 * width])
    kk = k * keyk_ref[...]
    kk = kk / jnp.maximum(jnp.sqrt(_head_sum(kk * kk)), 1e-12)
    k2 = k * (1.0 + (a - 1.0) * keya_ref[...])
    av = -kk
    bv = kk * a
    if t_valid < tlen:
        live = lax.broadcasted_iota(jnp.int32, lw.shape, 0) % tlen < t_valid
        lw = jnp.where(live, lw, 0.0)
        av = jnp.where(live, av, 0.0)
        bv = jnp.where(live, bv, 0.0)
        k2s = jnp.where(live, k2, 0.0)
        vs = jnp.where(live, v, 0.0)
    else:
        k2s, vs = k2, v

    ri = lax.broadcasted_iota(jnp.int32, (r2, r2), 0)
    ci = lax.broadcasted_iota(jnp.int32, (r2, r2), 1)
    same = (ri // tlen) == (ci // tlen)
    strict = same & (ci < ri)
    incl = same & (ci <= ri)
    eye = (ri == ci).astype(F32)
    cum_lower = incl[0:rows, 0:rows]
    cum = jnp.concatenate([cum_lower, same[0:rows, 0:rows]], axis=0).astype(F32).astype(BF16)
    lo = lax.broadcasted_iota(jnp.int32, (rows, PAIR), 1) < HEAD_DIM
    nsteps = max(1, (tlen - 1).bit_length())

    def stack(xp):
        return jnp.concatenate([jnp.where(lo, xp, 0.0), jnp.where(lo, 0.0, xp)], axis=0)

    def seq_rows(x, q):
        if nseq == 1:
            return x
        return jnp.concatenate([x[q * tlen:(q + 1) * tlen], x[rows + q * tlen:rows + (q + 1) * tlen]], axis=0)

    def unseq(parts):
        if nseq == 1:
            return parts[0]
        return jnp.concatenate([x[0:tlen] for x in parts] + [x[tlen:2 * tlen] for x in parts], axis=0)

    probs = []
    for sub in range(nsub):
        sl = slice(sub * rows, (sub + 1) * rows)
        lwc = lw[sl]
        g0, g1, g2 = _split3(lwc)
        gg = _dot(cum, g0) + _dot(cum, g1) + _dot(cum, g2)
        g = gg[0:rows]
        g_end = gg[rows:r2]
        eg = jnp.exp(g)
        eng = jnp.exp(-g)
        egc = jnp.exp(g_end - g)
        at = av[sl] * jnp.exp(g - lwc)
        rt = r[sl] * eg
        bt = bv[sl] * eng
        kt = k2s[sl] * eng
        bh = bv[sl] * egc
        kh = k2s[sl] * egc
        e_end = jnp.exp(g_end)
        vc = vs[sl]
        for p in range(npair):
            ps = slice(p * PAIR, (p + 1) * PAIR)
            probs.append(dict(
                sub=sub, p=p,
                la=stack(at[:, ps]).astype(BF16), lr=stack(rt[:, ps]), lb=stack(bt[:, ps]).astype(BF16),
                lk=stack(kt[:, ps]).astype(BF16), lbh=stack(bh[:, ps]), lkh=stack(kh[:, ps]), lv=stack(vc[:, ps]),
                e_end=e_end[:, ps]))
    for pr in probs:
        pr["lrb"] = pr["lr"].astype(BF16)
        pr["lvb"] = pr["lv"].astype(BF16)
        sc = _dot_nt(jnp.concatenate([pr["la"], pr["lrb"]], axis=0),
                     jnp.concatenate([pr["lb"], pr["lk"]], axis=0))
        pr["pw"] = jnp.where(strict, sc[0:r2, 0:r2], 0.0)
        pr["dak"] = jnp.where(strict, sc[0:r2, r2:2 * r2], 0.0)
        pr["drb"] = jnp.where(incl, sc[r2:2 * r2, 0:r2], 0.0).astype(BF16)
        pr["drk"] = jnp.where(incl, sc[r2:2 * r2, r2:2 * r2], 0.0)
        pr["inv"] = eye + pr["pw"]
    for step in range(nsteps):
        for pr in probs:
            pb = pr["pw"].astype(BF16)
            if step == 0:
                if nsteps > 1:
                    pr["pw"] = _dot(pb, pb)
            elif step < nsteps - 1:
                px = _dot(pb, jnp.concatenate([pb, pr["inv"].astype(BF16)], axis=1))
                pr["pw"] = px[:, 0:r2]
                pr["inv"] = pr["inv"] + px[:, r2:2 * r2]
            else:
                pr["inv"] = pr["inv"] + _dot(pb, pr["inv"].astype(BF16))
    for pr in probs:
        pr["dv"] = _dot(jnp.concatenate([pr["dak"], pr["drk"]], axis=0).astype(BF16), pr["lvb"])
    for pr in probs:
        tw = _dot(pr["inv"].astype(BF16), jnp.concatenate([pr["la"], pr["dv"][0:r2].astype(BF16)], axis=1))
        pr["lw"] = tw[:, 0:PAIR]
        pr["lu0"] = tw[:, PAIR:2 * PAIR]

    y_chunks = []
    for sub in range(nsub):
        sub_probs = [pr for pr in probs if pr["sub"] == sub]
        for pr in sub_probs:
            pr["ws"] = []
            for q in range(nseq):
                key = (0 if carry else sub * nseq + q, pr["p"])
                lhs = jnp.concatenate([seq_rows(pr["lw"], q), seq_rows(pr["lr"], q)], axis=0).astype(BF16)
                pr["ws"].append(_dot_nt(lhs, state[key].astype(BF16)))
        for pr in sub_probs:
            lu = unseq([w[0:2 * tlen] for w in pr["ws"]]) + pr["lu0"]
            pr["lu"] = lu
            y2 = unseq([w[2 * tlen:4 * tlen] for w in pr["ws"]]) + _dot(pr["drb"], lu.astype(BF16)) + pr["dv"][r2:2 * r2]
            pr["y"] = y2[0:rows] + y2[rows:r2]
        for pr in sub_probs:
            for q in range(nseq):
                key = (0 if carry else sub * nseq + q, pr["p"])
                lhs = jnp.concatenate([seq_rows(pr["lu"], q), seq_rows(pr["lv"], q)], axis=0).astype(BF16)
                rhs = jnp.concatenate([seq_rows(pr["lbh"], q), seq_rows(pr["lkh"], q)], axis=0).astype(BF16)
                decay = pr["e_end"][q * tlen:q * tlen + 1, :]
                state[key] = state[key] * decay + _dot_tn(lhs, rhs)
        y_chunks.append(_cat([pr["y"] for pr in sub_probs], axis=-1))
    y = _cat(y_chunks, axis=0)

    inv_n = 1.0 / HEAD_DIM
    mean = _head_sum(y) * inv_n
    d = y - mean
    var = _head_sum(d * d) * inv_n
    yn = d * lax.rsqrt(var + GN_EPS) * lnw_ref[...] + lnb_ref[...]
    bonus = _head_sum(r * k2 * bonus_ref[...]) * v
    out_ref[...] = (yn + bonus) * _silu(gate_ref[...])

    views = [sout_ref.at[s] for s in range(sout_ref.shape[0])] if nslot else [sout_ref]

    def write_state(q, src):
        for p in range(npair):
            sp = src[(q, p)]
            for dst in views:
                dst[q, 2 * p] = sp[0:HEAD_DIM, 0:HEAD_DIM]
                dst[q, 2 * p + 1] = sp[HEAD_DIM:PAIR, HEAD_DIM:PAIR]

    if carry:
        for p in range(npair):
            s_scr[p] = state[(0, p)]

        @pl.when(tb == nblk - 1)
        def _():
            write_state(0, state)
    else:
        for q in range(nsub * nseq):
            write_state(q, state)


def _rwkv_sample(cols, gate, prev, s0, lp, *, nseq_total, rows, tlen, nsub, t_valid, layer, depth, stacked=()):
    m, ncols = cols.shape
    width = gate.shape[1]
    blk_rows = rows * nsub
    const = lambda b, t: (0, 0)
    vec = lambda n: pl.BlockSpec((1, n), const)
    hd = (width // HEAD_DIM, HEAD_DIM, HEAD_DIM)
    blk = lambda b, t: (b, 0)
    sblk = blk_rows // tlen
    if layer == 0:
        nslot, extra_specs, aliases = depth, [], {}
        sout_spec = pl.BlockSpec((depth, sblk) + hd, lambda b, t: (0, b, 0, 0, 0))
    else:
        nslot, extra_specs, aliases = 0, [pl.BlockSpec(memory_space=pl.ANY)], {N_RWKV_IN: 1}
        sout_spec = pl.BlockSpec((None, sblk) + hd, lambda b, t: (layer, b, 0, 0, 0))
    return pl.pallas_call(
        functools.partial(_rwkv_kernel, rows=rows, tlen=tlen, nsub=nsub, width=width, t_valid=t_valid, carry=False,
                          nslot=nslot),
        grid=(m // blk_rows, 1),
        in_specs=[pl.BlockSpec((blk_rows, ncols), blk), pl.BlockSpec((blk_rows, width), blk),
                  pl.BlockSpec((blk_rows, ncols), blk), pl.BlockSpec((sblk,) + hd, lambda b, t: (b, 0, 0, 0)),
                  vec(ncols), vec(width), pl.BlockSpec((PAIR, 2 * width), const), vec(width), vec(width),
                  vec(width), vec(width), vec(width), vec(width)] + extra_specs,
        out_specs=[pl.BlockSpec((blk_rows, width), blk), sout_spec],
        out_shape=[jax.ShapeDtypeStruct((m, width), F32), jax.ShapeDtypeStruct((depth, nseq_total) + hd, F32)],
        input_output_aliases=aliases,
        scratch_shapes=[pltpu.VMEM((width // PAIR, PAIR, PAIR), F32), pltpu.VMEM((1, ncols), F32)],
        compiler_params=_params(("arbitrary", "arbitrary")),
        name="rwkv_sample",
    )(cols, gate, prev, s0, lp["mu"], lp["w0"], lp["lora"], lp["a0"], lp["key_k"],
      lp["key_a"], lp["bonus"], lp["lnx_w"], lp["lnx_b"], *stacked)


def _lambda(lam_ref, lam_init):
    lam = lam_ref[...]
    e1 = jnp.exp(jnp.sum(lam[0:1] * lam[1:2], axis=-1, keepdims=True))
    e2 = jnp.exp(jnp.sum(lam[2:3] * lam[3:4], axis=-1, keepdims=True))
    return e1 - e2 + lam_init


def _subln_gate(o, subw, gate, lam_init):
    o = o * lax.rsqrt(jnp.mean(o * o, axis=-1, keepdims=True) + SUBLN_EPS) * subw
    return o * (1.0 - lam_init) * _silu(gate)


def _attn_prompt_kernel(q_ref, k_ref, vt_ref, dg_ref, subw_ref, lam_ref, out_ref, vx_scr, *, tq, lam_init):
    seq = q_ref.shape[0]
    lo = lax.broadcasted_iota(jnp.int32, (tq, PAIR), 1) < HEAD_DIM
    ki = lax.broadcasted_iota(jnp.int32, (tq, 2 * tq), 0)
    qi = lax.broadcasted_iota(jnp.int32, (tq, 2 * tq), 1)
    causal = ki <= jnp.where(qi >= tq, qi - tq, qi)
    lam = _lambda(lam_ref, lam_init)
    vx_scr[0:PAIR, :] = vt_ref[...]
    vx_scr[PAIR:PAIR + BF16_ROWS, :] = jnp.ones((BF16_ROWS, seq), BF16)
    def scores(i):
        past, here = i * tq, (i + 1) * tq
        q = q_ref[past:here, :]
        zero = jnp.zeros_like(q)
        qs = jnp.concatenate([jnp.where(lo, q, zero), jnp.where(lo, zero, q)], axis=0)
        s_diag = jnp.where(causal, _dot_nt(k_ref[past:here, :], qs), -jnp.inf)
        s_past = _dot_nt(k_ref[0:past, :], qs) if i > 0 else None
        return s_diag, s_past

    nq = seq // tq
    ahead = scores(0)
    for i in range(nq):
        past, here = i * tq, (i + 1) * tq
        s_diag, s_past = ahead
        if i + 1 < nq:
            ahead = scores(i + 1)
        m = jnp.max(s_diag, axis=0, keepdims=True)
        if i > 0:
            m = jnp.maximum(m, jnp.max(s_past, axis=0, keepdims=True))
        acc = _dot(vx_scr[:, past:here], jnp.exp2(s_diag - m).astype(BF16))
        if i > 0:
            acc = acc + _dot(vx_scr[:, 0:past], jnp.exp2(s_past - m).astype(BF16))
        ot = acc[0:PAIR] / acc[PAIR:PAIR + 1]
        ot = ot[:, 0:tq] - lam * ot[:, tq:2 * tq]
        out_ref[past:here, :] = _subln_gate(ot.T, subw_ref[...], dg_ref[past:here, :], lam_init)


def _attn_prompt(q, kb, vt, dg, subw, lam_vecs, *, batch, seq, tq, lam_init):
    m, width = q.shape
    blk = pl.BlockSpec((seq, PAIR), lambda b, h: (b, h))
    const = lambda b, h: (0, 0)
    return pl.pallas_call(
        functools.partial(_attn_prompt_kernel, tq=tq, lam_init=lam_init),
        grid=(batch, width // PAIR),
        in_specs=[blk, blk, pl.BlockSpec((None, PAIR, seq), lambda b, h: (b, h, 0)), blk,
                  pl.BlockSpec((1, PAIR), const), pl.BlockSpec((4, HEAD_DIM), const)],
        out_specs=blk,
        out_shape=jax.ShapeDtypeStruct((m, width), F32),
        scratch_shapes=[pltpu.VMEM((PAIR + BF16_ROWS, seq), BF16)],
        compiler_params=_params(("arbitrary", "arbitrary")),
        name="attn_prompt",
    )(q, kb, vt, dg, subw, lam_vecs)


def _attn_sample_steps(pt_ref, q_ref, kn_ref, vn_ref, dg_ref, subw_ref, lam_ref, ck_ref, cv_ref, out_ref,
                        kbuf, vbuf, sem, *, npages, tpad, heads, lam_init, nseq):
    b = pl.program_id(0) * pl.num_programs(1) + pl.program_id(1)
    prow = kbuf.shape[2]

    def fetch(seq, slot):
        out = []
        for j in range(npages):
            row = pl.multiple_of(pt_ref[seq * npages + j] * prow, prow)
            out.append(pltpu.make_async_copy(ck_ref.at[pl.ds(row, prow), :], kbuf.at[slot, j], sem.at[slot, 0]))
            out.append(pltpu.make_async_copy(cv_ref.at[pl.ds(row, prow), :], vbuf.at[slot, j], sem.at[slot, 1]))
        return out

    def wait(slot):
        for j in range(npages):
            pltpu.make_async_copy(ck_ref.at[pl.ds(0, prow), :], kbuf.at[slot, j], sem.at[slot, 0]).wait()
            pltpu.make_async_copy(cv_ref.at[pl.ds(0, prow), :], vbuf.at[slot, j], sem.at[slot, 1]).wait()

    @pl.when(b == 0)
    def _():
        for s in range(PAGE_SLOTS - 1):
            for c in fetch(min(s, nseq - 1), s):
                c.start()

    slot = b % PAGE_SLOTS
    wait(slot)
    nrow = 2 * heads * tpad
    half = nrow // 2
    q = q_ref[0] * (1.0 / math.sqrt(HEAD_DIM))
    lo = lax.broadcasted_iota(jnp.int32, (tpad, PAIR), 1) < HEAD_DIM
    pieces = []
    for c in range(2):
        for h in range(heads):
            qh = q[:, h * PAIR:(h + 1) * PAIR]
            pieces.append(jnp.where(lo, qh, 0.0) if c == 0 else jnp.where(lo, 0.0, qh))
    qst = jnp.concatenate(pieces, axis=0).astype(BF16)

    def two(buf, j):
        return jnp.concatenate([buf[slot, j].astype(BF16), buf[slot, j + 1].astype(BF16)], axis=0)

    rhead = (lax.broadcasted_iota(jnp.int32, (nrow, 2 * prow), 0) // tpad) % heads
    chead = lax.broadcasted_iota(jnp.int32, (nrow, 2 * prow), 1) % heads
    own = rhead == chead
    scores = [jnp.where(own, _dot_nt(qst, two(kbuf, j)), -jnp.inf) for j in range(0, npages, 2)]

    def new_rows(ref):
        x = ref[0]
        xs = [x[:, h * PAIR:(h + 1) * PAIR] for h in range(heads)]
        xs.append(jnp.zeros((LANES - heads * tpad, PAIR), F32))
        return jnp.concatenate(xs, axis=0).astype(BF16)

    s_new = _dot_nt(qst, new_rows(kn_ref))
    rn = lax.broadcasted_iota(jnp.int32, (nrow, LANES), 0)
    cn = lax.broadcasted_iota(jnp.int32, (nrow, LANES), 1)
    visible = ((rn // tpad) % heads == cn // tpad) & (cn % tpad <= rn % tpad) & (cn < heads * tpad)
    s_new = jnp.where(visible, s_new, -jnp.inf)
    yield
    m = jnp.max(s_new, axis=-1, keepdims=True)
    for s in scores:
        m = jnp.maximum(m, jnp.max(s, axis=-1, keepdims=True))
    p_new = jnp.exp(s_new - m)
    l = jnp.sum(p_new, axis=-1, keepdims=True)
    probs = []
    for s in scores:
        p = jnp.exp(s - m)
        l = l + jnp.sum(p, axis=-1, keepdims=True)
        probs.append(p)
    inv_l = 1.0 / l
    w1 = inv_l[0:half]
    w2 = _lambda(lam_ref, lam_init) * inv_l[half:nrow]

    def mix(p):
        return (p[0:half] * w1 - p[half:nrow] * w2).astype(BF16)

    o = _dot(mix(p_new), new_rows(vn_ref))
    for idx, j in enumerate(range(0, npages, 2)):
        o = o + _dot(mix(probs[idx]), two(vbuf, j))
    dg = dg_ref[0]
    outs = [_subln_gate(o[h * tpad:(h + 1) * tpad], subw_ref[...], dg[:, h * PAIR:(h + 1) * PAIR], lam_init)
            for h in range(heads)]
    out_ref[0] = jnp.concatenate(outs, axis=-1)

    ahead = jnp.minimum(b + PAGE_SLOTS - 1, nseq - 1)
    for c in fetch(ahead, (b + PAGE_SLOTS - 1) % PAGE_SLOTS):
        c.start()

    @pl.when(b == nseq - 1)
    def _():
        for s in range(1, PAGE_SLOTS):
            wait((nseq - 1 + s) % PAGE_SLOTS)


def _rwkv_attn_kernel(pt_ref, *refs, rwkv_kw, attn_kw):
    n0, n1 = N_RWKV_IN, N_RWKV_IN + N_ATTN_IN
    rwkv_out, attn_out = refs[n1:n1 + 2], refs[n1 + 2]
    rwkv_scr, attn_scr = refs[n1 + 3:n1 + 5], refs[n1 + 5:n1 + 8]
    attn = _attn_sample_steps(pt_ref, *refs[n0:n1], attn_out, *attn_scr, **attn_kw)
    next(attn)
    _rwkv_kernel(*refs[0:n0], *rwkv_out, *rwkv_scr, **rwkv_kw)
    for _ in attn:
        pass


def _rwkv_prompt_attn_sample(cols, gate, prev, s0, lp, q, kn, vn, dg, subw, lam_vecs, cache_k, cache_v, page_table,
                             *, nseq_total, rows, nsub, layer, lam_init):
    m, ncols = cols.shape
    width = gate.shape[1]
    blk_rows = rows * nsub
    nblk = m // nseq_total // blk_rows
    hd = (width // HEAD_DIM, HEAD_DIM, HEAD_DIM)
    batch, tpad, dwidth = q.shape
    assert nseq_total * nblk == batch
    npages = page_table.shape[1]
    n_phys, page, heads = cache_k.shape[1], cache_k.shape[2], cache_k.shape[3]
    prow = page * heads
    ck = cache_k.reshape(-1, PAIR)
    cv = cache_v.reshape(-1, PAIR)
    pt = page_table.reshape(-1).astype(jnp.int32) + layer * n_phys

    const = lambda b, t, pt: (0, 0)
    vec = lambda n: pl.BlockSpec((1, n), const)
    blk = lambda b, t, pt: (b * nblk + t, 0)
    state_spec = pl.BlockSpec((1,) + hd, lambda b, t, pt: (b, 0, 0, 0))
    tok = pl.BlockSpec((1, tpad, dwidth), lambda b, t, pt: (b * nblk + t, 0, 0))
    any_spec = pl.BlockSpec(memory_space=pl.ANY)
    grid_spec = pltpu.PrefetchScalarGridSpec(
        num_scalar_prefetch=1,
        grid=(nseq_total, nblk),
        in_specs=[pl.BlockSpec((blk_rows, ncols), blk), pl.BlockSpec((blk_rows, width), blk),
                  pl.BlockSpec((1, 1, ncols), lambda b, t, pt: (b, 0, 0)), state_spec,
                  vec(ncols), vec(width), pl.BlockSpec((PAIR, 2 * width), const), vec(width), vec(width),
                  vec(width), vec(width), vec(width), vec(width),
                  tok, tok, tok, tok, pl.BlockSpec((1, PAIR), const), pl.BlockSpec((4, HEAD_DIM), const),
                  any_spec, any_spec],
        out_specs=[pl.BlockSpec((blk_rows, width), blk), state_spec, tok],
        scratch_shapes=[pltpu.VMEM((width // PAIR, PAIR, PAIR), F32), pltpu.VMEM((1, ncols), F32),
                        pltpu.VMEM((PAGE_SLOTS, npages, prow, PAIR), F32),
                        pltpu.VMEM((PAGE_SLOTS, npages, prow, PAIR), F32),
                        pltpu.SemaphoreType.DMA((PAGE_SLOTS, 2))],
    )
    rwkv_kw = dict(rows=rows, tlen=rows, nsub=nsub, width=width, t_valid=rows, carry=True, nslot=0)
    attn_kw = dict(npages=npages, tpad=tpad, heads=heads, lam_init=lam_init, nseq=batch)
    return pl.pallas_call(
        functools.partial(_rwkv_attn_kernel, rwkv_kw=rwkv_kw, attn_kw=attn_kw),
        grid_spec=grid_spec,
        out_shape=[jax.ShapeDtypeStruct((m, width), F32), jax.ShapeDtypeStruct((nseq_total,) + hd, F32),
                   jax.ShapeDtypeStruct((batch, tpad, dwidth), F32)],
        compiler_params=_params(("arbitrary", "arbitrary")),
        name="rwkv_prompt_attn_sample",
    )(pt, cols, gate, prev.reshape(nseq_total, 1, ncols), s0, lp["mu"], lp["w0"], lp["lora"], lp["a0"], lp["key_k"],
      lp["key_a"], lp["bonus"], lp["lnx_w"], lp["lnx_b"], q, kn, vn, dg, subw, lam_vecs, ck, cv)


def _outproj_kernel(x_ref, r_ref, d_ref, w_ref, fw_ref, o_ref, *, final):
    half = r_ref.shape[1]
    y = x_ref[...] + _dot(r_ref[...].astype(BF16), w_ref[0:half, :]) + _dot(d_ref[...].astype(BF16), w_ref[half:, :])
    if final:
        y = y * lax.rsqrt(jnp.mean(y * y, axis=-1, keepdims=True) + RMS_EPS) * fw_ref[...]
    o_ref[...] = y


def _outproj(x2d, rw, df, w_bf16, final_w, *, final, tm):
    m, d = x2d.shape
    half = rw.shape[1]
    row = lambda i: (i, 0)
    const = lambda i: (0, 0)
    return pl.pallas_call(
        functools.partial(_outproj_kernel, final=final),
        grid=(m // tm,),
        in_specs=[pl.BlockSpec((tm, d), row), pl.BlockSpec((tm, half), row), pl.BlockSpec((tm, half), row),
                  pl.BlockSpec(w_bf16.shape, const), pl.BlockSpec((1, d), const)],
        out_specs=pl.BlockSpec((tm, d), row),
        out_shape=jax.ShapeDtypeStruct((m, d), F32),
        compiler_params=_params(("arbitrary",)),
        name="outproj",
    )(x2d, rw, df, w_bf16, final_w.reshape(1, d))


def kernel(x_prompt, x_sample, cache_k, cache_v, page_table, state_rwkv, state_shift, norm_w, w_in, shift_mu, decay_w0, decay_lora_b, iclr_a0, iclr_lora_b, key_k, key_a, bonus_r_k, lnx_w, lnx_b, lam_q1, lam_k1, lam_q2, lam_k2, subln_w, w_out, final_norm_w):
    depth = w_in.shape[0]
    bp, seq, d = x_prompt.shape
    bs, tdec, _ = x_sample.shape
    width = decay_w0.shape[-1]
    ncols = shift_mu.shape[-1]
    rank_w, rank_a = decay_lora_b.shape[1], iclr_lora_b.shape[1]
    heads_r = width // HEAD_DIM
    heads_d = cache_k.shape[3]
    dwidth = heads_d * cache_v.shape[4]
    dqk = heads_d * cache_k.shape[4]
    assert rank_w == HEAD_DIM and rank_a == HEAD_DIM and ncols == 3 * width + PAIR
    assert cache_k.shape[4] == PAIR and cache_v.shape[4] == PAIR
    c0 = ncols
    c1 = c0 + width
    c2 = c1 + dqk
    c3 = c2 + dqk
    c4 = c3 + dwidth
    c5 = c4 + dwidth
    bounds = (c0, c1, c2, c3, c4, c5)
    assert w_in.shape[2] == c5

    tpad = SUBLANES
    chunk = 64
    tile = 256
    assert tdec <= tpad
    xp = x_prompt.reshape(bp * seq, d)
    xs = jnp.pad(x_sample, ((0, 0), (0, tpad - tdec), (0, 0))).reshape(bs * tpad, d)

    kst = vst = sst = None
    outs = {n: [] for n in ("sp", "hp", "ks", "vs", "hs")}
    for l in range(depth):
        lam_init = 0.8 - 0.6 * math.exp(-0.3 * l)
        w_l = w_in[l].astype(BF16)
        wo_l = w_out[l].astype(BF16)
        zeros = jnp.zeros((rank_w, width), F32)
        lora = jnp.concatenate([jnp.concatenate([decay_lora_b[l], zeros], axis=1),
                                jnp.concatenate([zeros, iclr_lora_b[l]], axis=1)], axis=0).astype(BF16)
        lp = dict(mu=shift_mu[l].reshape(1, ncols), w0=decay_w0[l].reshape(1, width), lora=lora,
                  a0=iclr_a0[l].reshape(1, width), key_k=key_k[l].reshape(1, width), key_a=key_a[l].reshape(1, width),
                  bonus=bonus_r_k[l].reshape(1, width), lnx_w=lnx_w[l].reshape(1, width),
                  lnx_b=lnx_b[l].reshape(1, width))
        lam_vecs = jnp.stack([lam_q1[l], lam_k1[l], lam_q2[l], lam_k2[l]], axis=0)
        subw = subln_w[l].reshape(1, PAIR)
        last = l == depth - 1

        cols, rg, q, kb, vt, dg, kst, vst = _proj(xp, norm_w[l], w_l, bounds, tm=2 * tile, prompt=True, seq=seq,
                                                  layer=l, depth=depth, stacked=() if l == 0 else (kst, vst))
        cols_s, rg_s, q_s, k_s, v_s, dg_s = _proj(xs, norm_w[l], w_l, bounds, tm=tile, prompt=False)
        r3 = lambda t: t.reshape(bs, tpad, -1)

        rw, s_new, df_s = _rwkv_prompt_attn_sample(
            cols, rg, jnp.zeros((bp, ncols), F32), jnp.zeros((bp, heads_r, HEAD_DIM, HEAD_DIM), F32), lp,
            r3(q_s), r3(k_s), r3(v_s), r3(dg_s), subw, lam_vecs, cache_k, cache_v, page_table,
            nseq_total=bp, rows=chunk, nsub=2, layer=l, lam_init=lam_init)
        df = _attn_prompt(q, kb, vt, dg, subw, lam_vecs, batch=bp, seq=seq, tq=tile, lam_init=lam_init)
        xp = _outproj(xp, rw, df, wo_l, final_norm_w, final=last, tm=2 * tile)
        outs["sp"].append(s_new)
        outs["hp"].append(cols.reshape(bp, seq, ncols)[:, -1, :])

        prev = jnp.pad(state_shift[l][:, None, :], ((0, 0), (0, tpad - 1), (0, 0))).reshape(bs * tpad, ncols)
        rw_s, sst = _rwkv_sample(cols_s, rg_s, prev, state_rwkv[l], lp, nseq_total=bs, rows=chunk, tlen=tpad, nsub=1,
                                 t_valid=tdec, layer=l, depth=depth, stacked=() if l == 0 else (sst,))
        xs = _outproj(xs, rw_s, df_s.reshape(bs * tpad, dwidth), wo_l, final_norm_w, final=last, tm=tile)
        outs["ks"].append(r3(k_s)[:, :tdec].reshape(bs, tdec, heads_d, PAIR))
        outs["vs"].append(r3(v_s)[:, :tdec].reshape(bs, tdec, heads_d, PAIR))
        outs["hs"].append(r3(cols_s)[:, tdec - 1, :])

    y_prompt = xp.reshape(bp, seq, d)
    y_sample = xs.reshape(bs, tpad, d)[:, :tdec]
    st = lambda n: jnp.stack(outs[n], axis=0)
    k_prompt = kst.reshape(depth, bp, seq, heads_d, PAIR)
    v_prompt = vst.reshape(depth, bp, seq, heads_d, PAIR)
    return (y_prompt, y_sample, k_prompt, v_prompt, st("sp"), st("hp"), st("ks"), st("vs"), sst, st("hs"))
```

```python
import functools
import math

import jax
import jax.numpy as jnp
from jax import lax
from jax.experimental import pallas as pl
from jax.experimental.pallas import tpu as pltpu

F32 = jnp.float32
BF16 = jnp.bfloat16

RMS_EPS = 1e-6
SUBLN_EPS = 1e-5
GN_EPS = 64e-5

LANES = 128
SUBLANES = 8
HEAD_DIM = 64
PAIR = 2 * HEAD_DIM
VMEM_LIMIT = 48 * 1024 * 1024
BF16_ROWS = 16
PAGE_SLOTS = 3
PROMPT_Q_SCALE = math.log2(math.e) / math.sqrt(HEAD_DIM)

_NT = (((1,), (1,)), ((), ()))
_TN = (((0,), (0,)), ((), ()))


def _dot(a, b):
    return jnp.dot(a, b, preferred_element_type=F32)


def _dot_nt(a, b):
    return lax.dot_general(a, b, _NT, preferred_element_type=F32)


def _dot_tn(a, b):
    return lax.dot_general(a, b, _TN, preferred_element_type=F32)


def _sigmoid(x):
    return 1.0 / (1.0 + jnp.exp(-x))


def _silu(x):
    return x * _sigmoid(x)


def _cat(xs, axis):
    return xs[0] if len(xs) == 1 else jnp.concatenate(xs, axis=axis)


def _params(sem):
    return pltpu.CompilerParams(dimension_semantics=sem, vmem_limit_bytes=VMEM_LIMIT)


def _store_token_head(dst, val, heads):
    rows = val.shape[0]
    for h in range(heads):
        dst[pl.ds(h, rows, stride=heads), :] = val[:, h * PAIR:(h + 1) * PAIR]


def _proj_kernel(x_ref, nw_ref, w_ref, *refs, bounds, prompt, layer):
    x = x_ref[...]
    h = x * lax.rsqrt(jnp.mean(x * x, axis=-1, keepdims=True) + RMS_EPS) * nw_ref[...]
    hb = h.astype(BF16)
    c0, c1, c2, c3, c4, c5 = bounds
    mm = lambda lo, hi: _dot(hb, w_ref[:, lo:hi])
    if prompt:
        cols_ref, rg_ref, q_ref, kb_ref, vt_ref, dg_ref, kst_ref, vst_ref = refs[-8:]
    else:
        cols_ref, rg_ref, q_ref, k_ref, v_ref, dg_ref = refs
    cols_ref[...] = mm(0, c0)
    rg_ref[...] = mm(c0, c1)
    q = mm(c1, c2)
    k = mm(c2, c3)
    v = mm(c3, c4)
    dg_ref[...] = mm(c4, c5)
    if not prompt:
        q_ref[...] = q
        k_ref[...] = k
        v_ref[...] = v
        return
    q_ref[...] = (q * PROMPT_Q_SCALE).astype(BF16)
    kb_ref[...] = k.astype(BF16)
    vt_ref[0] = v.T.astype(BF16)
    heads = (c3 - c2) // PAIR
    if layer == 0:
        for slot in range(kst_ref.shape[0]):
            _store_token_head(kst_ref.at[slot], k, heads)
            _store_token_head(vst_ref.at[slot], v, heads)
    else:
        _store_token_head(kst_ref, k, heads)
        _store_token_head(vst_ref, v, heads)


def _proj(x2d, norm_w, w_bf16, bounds, *, tm, prompt, seq=None, layer=0, depth=1, stacked=()):
    m, d = x2d.shape
    c0, c1, c2, c3, c4, c5 = bounds
    row = lambda i: (i, 0)
    const = lambda i: (0, 0)
    wide = lambda w, dt: (pl.BlockSpec((tm, w), row), jax.ShapeDtypeStruct((m, w), dt))
    in_specs = [pl.BlockSpec((tm, d), row), pl.BlockSpec((1, d), const),
                pl.BlockSpec(w_bf16.shape, const, pipeline_mode=pl.Buffered(1))]
    aliases = {}
    if prompt:
        nt = seq // tm
        heads = (c3 - c2) // PAIR
        outs = [wide(c0, F32), wide(c1 - c0, F32), wide(c2 - c1, BF16), wide(c3 - c2, BF16),
                (pl.BlockSpec((1, c4 - c3, tm), lambda i: (i // nt, 0, i % nt)),
                 jax.ShapeDtypeStruct((m // seq, c4 - c3, seq), BF16)),
                wide(c5 - c4, F32)]
        if layer == 0:
            st_spec = pl.BlockSpec((depth, tm * heads, PAIR), lambda i: (0, i, 0))
        else:
            st_spec = pl.BlockSpec((None, tm * heads, PAIR), lambda i: (layer, i, 0))
            in_specs += [pl.BlockSpec(memory_space=pl.ANY)] * 2
            aliases = {3: 6, 4: 7}
        outs += [(st_spec, jax.ShapeDtypeStruct((depth, m * heads, PAIR), F32))] * 2
    else:
        outs = [wide(c0, F32), wide(c1 - c0, F32), wide(c2 - c1, F32), wide(c3 - c2, F32), wide(c4 - c3, F32),
                wide(c5 - c4, F32)]
    return pl.pallas_call(
        functools.partial(_proj_kernel, bounds=bounds, prompt=prompt, layer=layer),
        grid=(m // tm,),
        in_specs=in_specs,
        out_specs=[o[0] for o in outs],
        out_shape=[o[1] for o in outs],
        input_output_aliases=aliases,
        compiler_params=_params(("arbitrary",)),
        name="proj_prompt" if prompt else "proj_sample",
    )(x2d, norm_w.reshape(1, d), w_bf16, *stacked)


def _head_sum(x):
    rows, width = x.shape
    lo = lax.broadcasted_iota(jnp.int32, (rows, PAIR), 1) < HEAD_DIM
    out = []
    for p in range(width // PAIR):
        xp = x[:, p * PAIR:(p + 1) * PAIR]
        s_lo = jnp.sum(jnp.where(lo, xp, 0.0), axis=-1, keepdims=True)
        s_hi = jnp.sum(jnp.where(lo, 0.0, xp), axis=-1, keepdims=True)
        out.append(jnp.where(lo, s_lo, s_hi))
    return jnp.concatenate(out, axis=-1)


def _split3(x):
    h0 = x.astype(BF16)
    r1 = x - h0.astype(F32)
    h1 = r1.astype(BF16)
    h2 = (r1 - h1.astype(F32)).astype(BF16)
    return h0, h1, h2


N_RWKV_IN = 13
N_ATTN_IN = 8


def _blockdiag(s_even, s_odd):
    zero = jnp.zeros((HEAD_DIM, HEAD_DIM), F32)
    return jnp.concatenate([jnp.concatenate([s_even, zero], axis=1), jnp.concatenate([zero, s_odd], axis=1)], axis=0)


def _rwkv_kernel(cols_ref, gate_ref, prev_ref, s0_ref, mu_ref, w0_ref, lora_ref, a0_ref, keyk_ref, keya_ref,
                 bonus_ref, lnw_ref, lnb_ref, *refs, rows, tlen, nsub, width, t_valid, carry, nslot):
    out_ref, sout_ref, s_scr, prev_scr = refs[-4:]
    tb = pl.program_id(1)
    nblk = pl.num_programs(1)
    npair = width // PAIR
    nseq = rows // tlen
    r2 = 2 * rows
    blk_rows = rows * nsub

    c = cols_ref[...]
    row_t = lax.broadcasted_iota(jnp.int32, c.shape, 0) % tlen if not carry else None
    if carry:
        @pl.when(tb == 0)
        def _():
            prev_scr[...] = prev_ref[0]
            for p in range(npair):
                s_scr[p] = _blockdiag(s0_ref[0, 2 * p], s0_ref[0, 2 * p + 1])

        row = lax.broadcasted_iota(jnp.int32, c.shape, 0)
        shifted = jnp.where(row == 0, prev_scr[...], pltpu.roll(c, 1, 0))
        prev_scr[...] = c[blk_rows - 1:blk_rows, :]
        state = {(0, p): s_scr[p] for p in range(npair)}
    else:
        shifted = jnp.where(row_t == 0, prev_ref[...], pltpu.roll(c, 1, 0))
        state = {(q, p): _blockdiag(s0_ref[q, 2 * p], s0_ref[q, 2 * p + 1])
                 for q in range(nsub * nseq) for p in range(npair)}

    xs = c + (shifted - c) * mu_ref[...]
    r = xs[:, 0:width]
    k = xs[:, width:2 * width]
    v = xs[:, 2 * width:3 * width]
    xwa = xs[:, 3 * width:3 * width + PAIR]
    lo_rank = lax.broadcasted_iota(jnp.int32, xwa.shape, 1) < HEAD_DIM
    xwa = jnp.where(lo_rank, jnp.tanh(xwa), xwa)
    lora = _dot(xwa.astype(BF16), lora_ref[...])
    z = -(w0_ref[...] + lora[:, 0:width])
    softplus = jnp.maximum(z, 0.0) + jnp.log(1.0 + jnp.exp(-jnp.abs(z)))
    lw = -jnp.exp(-softplus - 0.5)
    a = _sigmoid(a0_ref[...] + lora[:, width:2 * width])
    kk = k * keyk_ref[...]
    kk = kk / jnp.maximum(jnp.sqrt(_head_sum(kk * kk)), 1e-12)
    k2 = k * (1.0 + (a - 1.0) * keya_ref[...])
    av = -kk
    bv = kk * a
    if t_valid < tlen:
        live = lax.broadcasted_iota(jnp.int32, lw.shape, 0) % tlen < t_valid
        lw = jnp.where(live, lw, 0.0)
        av = jnp.where(live, av, 0.0)
        bv = jnp.where(live, bv, 0.0)
        k2s = jnp.where(live, k2, 0.0)
        vs = jnp.where(live, v, 0.0)
    else:
        k2s, vs = k2, v

    ri = lax.broadcasted_iota(jnp.int32, (r2, r2), 0)
    ci = lax.broadcasted_iota(jnp.int32, (r2, r2), 1)
    same = (ri // tlen) == (ci // tlen)
    strict = same & (ci < ri)
    incl = same & (ci <= ri)
    eye = (ri == ci).astype(F32)
    cum_lower = incl[0:rows, 0:rows]
    cum = jnp.concatenate([cum_lower, same[0:rows, 0:rows]], axis=0).astype(F32).astype(BF16)
    lo = lax.broadcasted_iota(jnp.int32, (rows, PAIR), 1) < HEAD_DIM
    nsteps = max(1, (tlen - 1).bit_length())

    def stack(xp):
        return jnp.concatenate([jnp.where(lo, xp, 0.0), jnp.where(lo, 0.0, xp)], axis=0)

    def seq_rows(x, q):
        if nseq == 1:
            return x
        return jnp.concatenate([x[q * tlen:(q + 1) * tlen], x[rows + q * tlen:rows + (q + 1) * tlen]], axis=0)

    def unseq(parts):
        if nseq == 1:
            return parts[0]
        return jnp.concatenate([x[0:tlen] for x in parts] + [x[tlen:2 * tlen] for x in parts], axis=0)

    probs = []
    for sub in range(nsub):
        sl = slice(sub * rows, (sub + 1) * rows)
        lwc = lw[sl]
        g0, g1, g2 = _split3(lwc)
        gg = _dot(cum, g0) + _dot(cum, g1) + _dot(cum, g2)
        g = gg[0:rows]
        g_end = gg[rows:r2]
        eg = jnp.exp(g)
        eng = jnp.exp(-g)
        egc = jnp.exp(g_end - g)
        at = av[sl] * jnp.exp(g - lwc)
        rt = r[sl] * eg
        bt = bv[sl] * eng
        kt = k2s[sl] * eng
        bh = bv[sl] * egc
        kh = k2s[sl] * egc
        e_end = jnp.exp(g_end)
        vc = vs[sl]
        for p in range(npair):
            ps = slice(p * PAIR, (p + 1) * PAIR)
            probs.append(dict(
                sub=sub, p=p,
                la=stack(at[:, ps]).astype(BF16), lr=stack(rt[:, ps]), lb=stack(bt[:, ps]).astype(BF16),
                lk=stack(kt[:, ps]).astype(BF16), lbh=stack(bh[:, ps]), lkh=stack(kh[:, ps]), lv=stack(vc[:, ps]),
                e_end=e_end[:, ps]))
    for pr in probs:
        pr["lrb"] = pr["lr"].astype(BF16)
        pr["lvb"] = pr["lv"].astype(BF16)
        sc = _dot_nt(jnp.concatenate([pr["la"], pr["lrb"]], axis=0),
                     jnp.concatenate([pr["lb"], pr["lk"]], axis=0))
        pr["pw"] = jnp.where(strict, sc[0:r2, 0:r2], 0.0)
        pr["dak"] = jnp.where(strict, sc[0:r2, r2:2 * r2], 0.0)
        pr["drb"] = jnp.where(incl, sc[r2:2 * r2, 0:r2], 0.0).astype(BF16)
        pr["drk"] = jnp.where(incl, sc[r2:2 * r2, r2:2 * r2], 0.0)
        pr["inv"] = eye + pr["pw"]
    for step in range(nsteps):
        for pr in probs:
            pb = pr["pw"].astype(BF16)
            if step == 0:
                if nsteps > 1:
                    pr["pw"] = _dot(pb, pb)
            elif step < nsteps - 1:
                px = _dot(pb, jnp.concatenate([pb, pr["inv"].astype(BF16)], axis=1))
                pr["pw"] = px[:, 0:r2]
                pr["inv"] = pr["inv"] + px[:, r2:2 * r2]
            else:
                pr["inv"] = pr["inv"] + _dot(pb, pr["inv"].astype(BF16))
    for pr in probs:
        pr["dv"] = _dot(jnp.concatenate([pr["dak"], pr["drk"]], axis=0).astype(BF16), pr["lvb"])
    for pr in probs:
        tw = _dot(pr["inv"].astype(BF16), jnp.concatenate([pr["la"], pr["dv"][0:r2].astype(BF16)], axis=1))
        pr["lw"] = tw[:, 0:PAIR]
        pr["lu0"] = tw[:, PAIR:2 * PAIR]

    y_chunks = []
    for sub in range(nsub):
        sub_probs = [pr for pr in probs if pr["sub"] == sub]
        for pr in sub_probs:
            pr["ws"] = []
            for q in range(nseq):
                key = (0 if carry else sub * nseq + q, pr["p"])
                lhs = jnp.concatenate([seq_rows(pr["lw"], q), seq_rows(pr["lr"], q)], axis=0).astype(BF16)
                pr["ws"].append(_dot_nt(lhs, state[key].astype(BF16)))
        for pr in sub_probs:
            lu = unseq([w[0:2 * tlen] for w in pr["ws"]]) + pr["lu0"]
            pr["lu"] = lu
            y2 = unseq([w[2 * tlen:4 * tlen] for w in pr["ws"]]) + _dot(pr["drb"], lu.astype(BF16)) + pr["dv"][r2:2 * r2]
            pr["y"] = y2[0:rows] + y2[rows:r2]
        for pr in sub_probs:
            for q in range(nseq):
                key = (0 if carry else sub * nseq + q, pr["p"])
                lhs = jnp.concatenate([seq_rows(pr["lu"], q), seq_rows(pr["lv"], q)], axis=0).astype(BF16)
                rhs = jnp.concatenate([seq_rows(pr["lbh"], q), seq_rows(pr["lkh"], q)], axis=0).astype(BF16)
                decay = pr["e_end"][q * tlen:q * tlen + 1, :]
                state[key] = state[key] * decay + _dot_tn(lhs, rhs)
        y_chunks.append(_cat([pr["y"] for pr in sub_probs], axis=-1))
    y = _cat(y_chunks, axis=0)

    inv_n = 1.0 / HEAD_DIM
    mean = _head_sum(y) * inv_n
    d = y - mean
    var = _head_sum(d * d) * inv_n
    yn = d * lax.rsqrt(var + GN_EPS) * lnw_ref[...] + lnb_ref[...]
    bonus = _head_sum(r * k2 * bonus_ref[...]) * v
    out_ref[...] = (yn + bonus) * _silu(gate_ref[...])

    views = [sout_ref.at[s] for s in range(sout_ref.shape[0])] if nslot else [sout_ref]

    def write_state(q, src):
        for p in range(npair):
            sp = src[(q, p)]
            for dst in views:
                dst[q, 2 * p] = sp[0:HEAD_DIM, 0:HEAD_DIM]
                dst[q, 2 * p + 1] = sp[HEAD_DIM:PAIR, HEAD_DIM:PAIR]

    if carry:
        for p in range(npair):
            s_scr[p] = state[(0, p)]

        @pl.when(tb == nblk - 1)
        def _():
            write_state(0, state)
    else:
        for q in range(nsub * nseq):
            write_state(q, state)


def _rwkv_sample(cols, gate, prev, s0, lp, *, nseq_total, rows, tlen, nsub, t_valid, layer, depth, stacked=()):
    m, ncols = cols.shape
    width = gate.shape[1]
    blk_rows = rows * nsub
    const = lambda b, t: (0, 0)
    vec = lambda n: pl.BlockSpec((1, n), const)
    hd = (width // HEAD_DIM, HEAD_DIM, HEAD_DIM)
    blk = lambda b, t: (b, 0)
    sblk = blk_rows // tlen
    if layer == 0:
        nslot, extra_specs, aliases = depth, [], {}
        sout_spec = pl.BlockSpec((depth, sblk) + hd, lambda b, t: (0, b, 0, 0, 0))
    else:
        nslot, extra_specs, aliases = 0, [pl.BlockSpec(memory_space=pl.ANY)], {N_RWKV_IN: 1}
        sout_spec = pl.BlockSpec((None, sblk) + hd, lambda b, t: (layer, b, 0, 0, 0))
    return pl.pallas_call(
        functools.partial(_rwkv_kernel, rows=rows, tlen=tlen, nsub=nsub, width=width, t_valid=t_valid, carry=False,
                          nslot=nslot),
        grid=(m // blk_rows, 1),
        in_specs=[pl.BlockSpec((blk_rows, ncols), blk), pl.BlockSpec((blk_rows, width), blk),
                  pl.BlockSpec((blk_rows, ncols), blk), pl.BlockSpec((sblk,) + hd, lambda b, t: (b, 0, 0, 0)),
                  vec(ncols), vec(width), pl.BlockSpec((PAIR, 2 * width), const), vec(width), vec(width),
                  vec(width), vec(width), vec(width), vec(width)] + extra_specs,
        out_specs=[pl.BlockSpec((blk_rows, width), blk), sout_spec],
        out_shape=[jax.ShapeDtypeStruct((m, width), F32), jax.ShapeDtypeStruct((depth, nseq_total) + hd, F32)],
        input_output_aliases=aliases,
        scratch_shapes=[pltpu.VMEM((width // PAIR, PAIR, PAIR), F32), pltpu.VMEM((1, ncols), F32)],
        compiler_params=_params(("arbitrary", "arbitrary")),
        name="rwkv_sample",
    )(cols, gate, prev, s0, lp["mu"], lp["w0"], lp["lora"], lp["a0"], lp["key_k"],
      lp["key_a"], lp["bonus"], lp["lnx_w"], lp["lnx_b"], *stacked)


def _lambda(lam_ref, lam_init):
    lam = lam_ref[...]
    e1 = jnp.exp(jnp.sum(lam[0:1] * lam[1:2], axis=-1, keepdims=True))
    e2 = jnp.exp(jnp.sum(lam[2:3] * lam[3:4], axis=-1, keepdims=True))
    return e1 - e2 + lam_init


def _subln_gate(o, subw, gate, lam_init):
    o = o * lax.rsqrt(jnp.mean(o * o, axis=-1, keepdims=True) + SUBLN_EPS) * subw
    return o * (1.0 - lam_init) * _silu(gate)


def _attn_prompt_kernel(q_ref, k_ref, vt_ref, dg_ref, subw_ref, lam_ref, out_ref, vx_scr, *, tq, lam_init):
    seq = q_ref.shape[0]
    lo = lax.broadcasted_iota(jnp.int32, (tq, PAIR), 1) < HEAD_DIM
    ki = lax.broadcasted_iota(jnp.int32, (tq, 2 * tq), 0)
    qi = lax.broadcasted_iota(jnp.int32, (tq, 2 * tq), 1)
    causal = ki <= jnp.where(qi >= tq, qi - tq, qi)
    lam = _lambda(lam_ref, lam_init)
    vx_scr[0:PAIR, :] = vt_ref[...]
    vx_scr[PAIR:PAIR + BF16_ROWS, :] = jnp.ones((BF16_ROWS, seq), BF16)
    def scores(i):
        past, here = i * tq, (i + 1) * tq
        q = q_ref[past:here, :]
        zero = jnp.zeros_like(q)
        qs = jnp.concatenate([jnp.where(lo, q, zero), jnp.where(lo, zero, q)], axis=0)
        s_diag = jnp.where(causal, _dot_nt(k_ref[past:here, :], qs), -jnp.inf)
        s_past = _dot_nt(k_ref[0:past, :], qs) if i > 0 else None
        return s_diag, s_past

    nq = seq // tq
    ahead = scores(0)
    for i in range(nq):
        past, here = i * tq, (i + 1) * tq
        s_diag, s_past = ahead
        if i + 1 < nq:
            ahead = scores(i + 1)
        m = jnp.max(s_diag, axis=0, keepdims=True)
        if i > 0:
            m = jnp.maximum(m, jnp.max(s_past, axis=0, keepdims=True))
        acc = _dot(vx_scr[:, past:here], jnp.exp2(s_diag - m).astype(BF16))
        if i > 0:
            acc = acc + _dot(vx_scr[:, 0:past], jnp.exp2(s_past - m).astype(BF16))
        ot = acc[0:PAIR] / acc[PAIR:PAIR + 1]
        ot = ot[:, 0:tq] - lam * ot[:, tq:2 * tq]
        out_ref[past:here, :] = _subln_gate(ot.T, subw_ref[...], dg_ref[past:here, :], lam_init)


def _attn_prompt(q, kb, vt, dg, subw, lam_vecs, *, batch, seq, tq, lam_init):
    m, width = q.shape
    blk = pl.BlockSpec((seq, PAIR), lambda b, h: (b, h))
    const = lambda b, h: (0, 0)
    return pl.pallas_call(
        functools.partial(_attn_prompt_kernel, tq=tq, lam_init=lam_init),
        grid=(batch, width // PAIR),
        in_specs=[blk, blk, pl.BlockSpec((None, PAIR, seq), lambda b, h: (b, h, 0)), blk,
                  pl.BlockSpec((1, PAIR), const), pl.BlockSpec((4, HEAD_DIM), const)],
        out_specs=blk,
        out_shape=jax.ShapeDtypeStruct((m, width), F32),
        scratch_shapes=[pltpu.VMEM((PAIR + BF16_ROWS, seq), BF16)],
        compiler_params=_params(("arbitrary", "arbitrary")),
        name="attn_prompt",
    )(q, kb, vt, dg, subw, lam_vecs)


def _attn_sample_steps(pt_ref, q_ref, kn_ref, vn_ref, dg_ref, subw_ref, lam_ref, ck_ref, cv_ref, out_ref,
                        kbuf, vbuf, sem, *, b, i, first, last, npages, tpad, heads, lam_init, nseq):
    prow = kbuf.shape[2]

    def fetch(seq, slot):
        out = []
        for j in range(npages):
            row = pl.multiple_of(pt_ref[seq * npages + j] * prow, prow)
            out.append(pltpu.make_async_copy(ck_ref.at[pl.ds(row, prow), :], kbuf.at[slot, j], sem.at[slot, 0]))
            out.append(pltpu.make_async_copy(cv_ref.at[pl.ds(row, prow), :], vbuf.at[slot, j], sem.at[slot, 1]))
        return out

    def wait(slot):
        for j in range(npages):
            pltpu.make_async_copy(ck_ref.at[pl.ds(0, prow), :], kbuf.at[slot, j], sem.at[slot, 0]).wait()
            pltpu.make_async_copy(cv_ref.at[pl.ds(0, prow), :], vbuf.at[slot, j], sem.at[slot, 1]).wait()

    if first:
        @pl.when(b == 0)
        def _():
            for s in range(PAGE_SLOTS - 1):
                for c in fetch(min(s, nseq - 1), s):
                    c.start()

    slot = b % PAGE_SLOTS
    wait(slot)
    nrow = 2 * heads * tpad
    half = nrow // 2
    q = q_ref[i] * (1.0 / math.sqrt(HEAD_DIM))
    lo = lax.broadcasted_iota(jnp.int32, (tpad, PAIR), 1) < HEAD_DIM
    pieces = []
    for c in range(2):
        for h in range(heads):
            qh = q[:, h * PAIR:(h + 1) * PAIR]
            pieces.append(jnp.where(lo, qh, 0.0) if c == 0 else jnp.where(lo, 0.0, qh))
    qst = jnp.concatenate(pieces, axis=0).astype(BF16)

    def two(buf, j):
        return jnp.concatenate([buf[slot, j].astype(BF16), buf[slot, j + 1].astype(BF16)], axis=0)

    rhead = (lax.broadcasted_iota(jnp.int32, (nrow, 2 * prow), 0) // tpad) % heads
    chead = lax.broadcasted_iota(jnp.int32, (nrow, 2 * prow), 1) % heads
    own = rhead == chead
    scores = [jnp.where(own, _dot_nt(qst, two(kbuf, j)), -jnp.inf) for j in range(0, npages, 2)]

    def new_rows(ref):
        x = ref[i]
        xs = [x[:, h * PAIR:(h + 1) * PAIR] for h in range(heads)]
        xs.append(jnp.zeros((LANES - heads * tpad, PAIR), F32))
        return jnp.concatenate(xs, axis=0).astype(BF16)

    s_new = _dot_nt(qst, new_rows(kn_ref))
    rn = lax.broadcasted_iota(jnp.int32, (nrow, LANES), 0)
    cn = lax.broadcasted_iota(jnp.int32, (nrow, LANES), 1)
    visible = ((rn // tpad) % heads == cn // tpad) & (cn % tpad <= rn % tpad) & (cn < heads * tpad)
    s_new = jnp.where(visible, s_new, -jnp.inf)
    yield
    m = jnp.max(s_new, axis=-1, keepdims=True)
    for s in scores:
        m = jnp.maximum(m, jnp.max(s, axis=-1, keepdims=True))
    p_new = jnp.exp(s_new - m)
    l = jnp.sum(p_new, axis=-1, keepdims=True)
    probs = []
    for s in scores:
        p = jnp.exp(s - m)
        l = l + jnp.sum(p, axis=-1, keepdims=True)
        probs.append(p)
    inv_l = 1.0 / l
    w1 = inv_l[0:half]
    w2 = _lambda(lam_ref, lam_init) * inv_l[half:nrow]

    def mix(p):
        return (p[0:half] * w1 - p[half:nrow] * w2).astype(BF16)

    o = _dot(mix(p_new), new_rows(vn_ref))
    for idx, j in enumerate(range(0, npages, 2)):
        o = o + _dot(mix(probs[idx]), two(vbuf, j))
    dg = dg_ref[i]
    outs = [_subln_gate(o[h * tpad:(h + 1) * tpad], subw_ref[...], dg[:, h * PAIR:(h + 1) * PAIR], lam_init)
            for h in range(heads)]
    out_ref[i] = jnp.concatenate(outs, axis=-1)

    ahead = jnp.minimum(b + PAGE_SLOTS - 1, nseq - 1)
    for c in fetch(ahead, (b + PAGE_SLOTS - 1) % PAGE_SLOTS):
        c.start()

    if last:
        @pl.when(b == nseq - 1)
        def _():
            for s in range(1, PAGE_SLOTS):
                wait((nseq - 1 + s) % PAGE_SLOTS)


def _rwkv_attn_kernel(pt_ref, *refs, rwkv_kw, attn_kw):
    n0, n1 = N_RWKV_IN, N_RWKV_IN + N_ATTN_IN
    rwkv_out, attn_out = refs[n1:n1 + 2], refs[n1 + 2]
    rwkv_scr, attn_scr = refs[n1 + 3:n1 + 5], refs[n1 + 5:n1 + 8]
    nper = attn_out.shape[0]
    step = pl.program_id(0) * pl.num_programs(1) + pl.program_id(1)
    attn = [_attn_sample_steps(pt_ref, *refs[n0:n1], attn_out, *attn_scr, b=step * nper + i, i=i, first=i == 0,
                               last=i == nper - 1, **attn_kw) for i in range(nper)]
    next(attn[0])
    _rwkv_kernel(*refs[0:n0], *rwkv_out, *rwkv_scr, **rwkv_kw)
    for steps in attn:
        for _ in steps:
            pass


def _rwkv_prompt_attn_sample(cols, gate, prev, s0, lp, q, kn, vn, dg, subw, lam_vecs, cache_k, cache_v, page_table,
                             *, nseq_total, rows, nsub, layer, lam_init):
    m, ncols = cols.shape
    width = gate.shape[1]
    blk_rows = rows * nsub
    nblk = m // nseq_total // blk_rows
    hd = (width // HEAD_DIM, HEAD_DIM, HEAD_DIM)
    batch, tpad, dwidth = q.shape
    nper = batch // (nseq_total * nblk)
    assert nper * nseq_total * nblk == batch
    npages = page_table.shape[1]
    n_phys, page, heads = cache_k.shape[1], cache_k.shape[2], cache_k.shape[3]
    prow = page * heads
    ck = cache_k.reshape(-1, PAIR)
    cv = cache_v.reshape(-1, PAIR)
    pt = page_table.reshape(-1).astype(jnp.int32) + layer * n_phys

    const = lambda b, t, pt: (0, 0)
    vec = lambda n: pl.BlockSpec((1, n), const)
    blk = lambda b, t, pt: (b * nblk + t, 0)
    state_spec = pl.BlockSpec((1,) + hd, lambda b, t, pt: (b, 0, 0, 0))
    tok = pl.BlockSpec((nper, tpad, dwidth), lambda b, t, pt: (b * nblk + t, 0, 0))
    any_spec = pl.BlockSpec(memory_space=pl.ANY)
    grid_spec = pltpu.PrefetchScalarGridSpec(
        num_scalar_prefetch=1,
        grid=(nseq_total, nblk),
        in_specs=[pl.BlockSpec((blk_rows, ncols), blk), pl.BlockSpec((blk_rows, width), blk),
                  pl.BlockSpec((1, 1, ncols), lambda b, t, pt: (b, 0, 0)), state_spec,
                  vec(ncols), vec(width), pl.BlockSpec((PAIR, 2 * width), const), vec(width), vec(width),
                  vec(width), vec(width), vec(width), vec(width),
                  tok, tok, tok, tok, pl.BlockSpec((1, PAIR), const), pl.BlockSpec((4, HEAD_DIM), const),
                  any_spec, any_spec],
        out_specs=[pl.BlockSpec((blk_rows, width), blk), state_spec, tok],
        scratch_shapes=[pltpu.VMEM((width // PAIR, PAIR, PAIR), F32), pltpu.VMEM((1, ncols), F32),
                        pltpu.VMEM((PAGE_SLOTS, npages, prow, PAIR), F32),
                        pltpu.VMEM((PAGE_SLOTS, npages, prow, PAIR), F32),
                        pltpu.SemaphoreType.DMA((PAGE_SLOTS, 2))],
    )
    rwkv_kw = dict(rows=rows, tlen=rows, nsub=nsub, width=width, t_valid=rows, carry=True, nslot=0)
    attn_kw = dict(npages=npages, tpad=tpad, heads=heads, lam_init=lam_init, nseq=batch)
    return pl.pallas_call(
        functools.partial(_rwkv_attn_kernel, rwkv_kw=rwkv_kw, attn_kw=attn_kw),
        grid_spec=grid_spec,
        out_shape=[jax.ShapeDtypeStruct((m, width), F32), jax.ShapeDtypeStruct((nseq_total,) + hd, F32),
                   jax.ShapeDtypeStruct((batch, tpad, dwidth), F32)],
        compiler_params=_params(("arbitrary", "arbitrary")),
        name="rwkv_prompt_attn_sample",
    )(pt, cols, gate, prev.reshape(nseq_total, 1, ncols), s0, lp["mu"], lp["w0"], lp["lora"], lp["a0"], lp["key_k"],
      lp["key_a"], lp["bonus"], lp["lnx_w"], lp["lnx_b"], q, kn, vn, dg, subw, lam_vecs, ck, cv)


def _outproj_kernel(x_ref, r_ref, d_ref, w_ref, fw_ref, o_ref, *, final):
    half = r_ref.shape[1]
    y = x_ref[...] + _dot(r_ref[...].astype(BF16), w_ref[0:half, :]) + _dot(d_ref[...].astype(BF16), w_ref[half:, :])
    if final:
        y = y * lax.rsqrt(jnp.mean(y * y, axis=-1, keepdims=True) + RMS_EPS) * fw_ref[...]
    o_ref[...] = y


def _outproj(x2d, rw, df, w_bf16, final_w, *, final, tm):
    m, d = x2d.shape
    half = rw.shape[1]
    row = lambda i: (i, 0)
    const = lambda i: (0, 0)
    return pl.pallas_call(
        functools.partial(_outproj_kernel, final=final),
        grid=(m // tm,),
        in_specs=[pl.BlockSpec((tm, d), row), pl.BlockSpec((tm, half), row), pl.BlockSpec((tm, half), row),
                  pl.BlockSpec(w_bf16.shape, const), pl.BlockSpec((1, d), const)],
        out_specs=pl.BlockSpec((tm, d), row),
        out_shape=jax.ShapeDtypeStruct((m, d), F32),
        compiler_params=_params(("arbitrary",)),
        name="outproj",
    )(x2d, rw, df, w_bf16, final_w.reshape(1, d))


def kernel(x_prompt, x_sample, cache_k, cache_v, page_table, state_rwkv, state_shift, norm_w, w_in, shift_mu, decay_w0, decay_lora_b, iclr_a0, iclr_lora_b, key_k, key_a, bonus_r_k, lnx_w, lnx_b, lam_q1, lam_k1, lam_q2, lam_k2, subln_w, w_out, final_norm_w):
    depth = w_in.shape[0]
    bp, seq, d = x_prompt.shape
    bs, tdec, _ = x_sample.shape
    width = decay_w0.shape[-1]
    ncols = shift_mu.shape[-1]
    rank_w, rank_a = decay_lora_b.shape[1], iclr_lora_b.shape[1]
    heads_r = width // HEAD_DIM
    heads_d = cache_k.shape[3]
    dwidth = heads_d * cache_v.shape[4]
    dqk = heads_d * cache_k.shape[4]
    assert rank_w == HEAD_DIM and rank_a == HEAD_DIM and ncols == 3 * width + PAIR
    assert cache_k.shape[4] == PAIR and cache_v.shape[4] == PAIR
    c0 = ncols
    c1 = c0 + width
    c2 = c1 + dqk
    c3 = c2 + dqk
    c4 = c3 + dwidth
    c5 = c4 + dwidth
    bounds = (c0, c1, c2, c3, c4, c5)
    assert w_in.shape[2] == c5

    tpad = SUBLANES
    chunk = 64
    tile = 256
    assert tdec <= tpad
    xp = x_prompt.reshape(bp * seq, d)
    xs = jnp.pad(x_sample, ((0, 0), (0, tpad - tdec), (0, 0))).reshape(bs * tpad, d)

    kst = vst = sst = None
    outs = {n: [] for n in ("sp", "hp", "ks", "vs", "hs")}
    for l in range(depth):
        lam_init = 0.8 - 0.6 * math.exp(-0.3 * l)
        w_l = w_in[l].astype(BF16)
        wo_l = w_out[l].astype(BF16)
        zeros = jnp.zeros((rank_w, width), F32)
        lora = jnp.concatenate([jnp.concatenate([decay_lora_b[l], zeros], axis=1),
                                jnp.concatenate([zeros, iclr_lora_b[l]], axis=1)], axis=0).astype(BF16)
        lp = dict(mu=shift_mu[l].reshape(1, ncols), w0=decay_w0[l].reshape(1, width), lora=lora,
                  a0=iclr_a0[l].reshape(1, width), key_k=key_k[l].reshape(1, width), key_a=key_a[l].reshape(1, width),
                  bonus=bonus_r_k[l].reshape(1, width), lnx_w=lnx_w[l].reshape(1, width),
                  lnx_b=lnx_b[l].reshape(1, width))
        lam_vecs = jnp.stack([lam_q1[l], lam_k1[l], lam_q2[l], lam_k2[l]], axis=0)
        subw = subln_w[l].reshape(1, PAIR)
        last = l == depth - 1

        cols, rg, q, kb, vt, dg, kst, vst = _proj(xp, norm_w[l], w_l, bounds, tm=2 * tile, prompt=True, seq=seq,
                                                  layer=l, depth=depth, stacked=() if l == 0 else (kst, vst))
        cols_s, rg_s, q_s, k_s, v_s, dg_s = _proj(xs, norm_w[l], w_l, bounds, tm=tile, prompt=False)
        r3 = lambda t: t.reshape(bs, tpad, -1)

        rw, s_new, df_s = _rwkv_prompt_attn_sample(
            cols, rg, jnp.zeros((bp, ncols), F32), jnp.zeros((bp, heads_r, HEAD_DIM, HEAD_DIM), F32), lp,
            r3(q_s), r3(k_s), r3(v_s), r3(dg_s), subw, lam_vecs, cache_k, cache_v, page_table,
            nseq_total=bp, rows=chunk, nsub=4, layer=l, lam_init=lam_init)
        df = _attn_prompt(q, kb, vt, dg, subw, lam_vecs, batch=bp, seq=seq, tq=tile, lam_init=lam_init)
        xp = _outproj(xp, rw, df, wo_l, final_norm_w, final=last, tm=2 * tile)
        outs["sp"].append(s_new)
        outs["hp"].append(cols.reshape(bp, seq, ncols)[:, -1, :])

        prev = jnp.pad(state_shift[l][:, None, :], ((0, 0), (0, tpad - 1), (0, 0))).reshape(bs * tpad, ncols)
        rw_s, sst = _rwkv_sample(cols_s, rg_s, prev, state_rwkv[l], lp, nseq_total=bs, rows=chunk, tlen=tpad, nsub=1,
                                 t_valid=tdec, layer=l, depth=depth, stacked=() if l == 0 else (sst,))
        xs = _outproj(xs, rw_s, df_s.reshape(bs * tpad, dwidth), wo_l, final_norm_w, final=last, tm=tile)
        outs["ks"].append(r3(k_s)[:, :tdec].reshape(bs, tdec, heads_d, PAIR))
        outs["vs"].append(r3(v_s)[:, :tdec].reshape(bs, tdec, heads_d, PAIR))
        outs["hs"].append(r3(cols_s)[:, tdec - 1, :])

    y_prompt = xp.reshape(bp, seq, d)
    y_sample = xs.reshape(bs, tpad, d)[:, :tdec]
    st = lambda n: jnp.stack(outs[n], axis=0)
    k_prompt = kst.reshape(depth, bp, seq, heads_d, PAIR)
    v_prompt = vst.reshape(depth, bp, seq, heads_d, PAIR)
    return (y_prompt, y_sample, k_prompt, v_prompt, st("sp"), st("hp"), st("ks"), st("vs"), sst, st("hs"))
```

```python
import functools
import math

import jax
import jax.numpy as jnp
from jax import lax
from jax.experimental import pallas as pl
from jax.experimental.pallas import tpu as pltpu

F32 = jnp.float32
BF16 = jnp.bfloat16

RMS_EPS = 1e-6
SUBLN_EPS = 1e-5
GN_EPS = 64e-5

LANES = 128
SUBLANES = 8
HEAD_DIM = 64
PAIR = 2 * HEAD_DIM
VMEM_LIMIT = 48 * 1024 * 1024
VMEM_LIMIT_FUSED = 58 * 1024 * 1024
BF16_ROWS = 16
PAGE_SLOTS = 3
PROMPT_Q_SCALE = math.log2(math.e) / math.sqrt(HEAD_DIM)

_NT = (((1,), (1,)), ((), ()))
_TN = (((0,), (0,)), ((), ()))


def _dot(a, b):
    return jnp.dot(a, b, preferred_element_type=F32)


def _dot_nt(a, b):
    return lax.dot_general(a, b, _NT, preferred_element_type=F32)


def _dot_tn(a, b):
    return lax.dot_general(a, b, _TN, preferred_element_type=F32)


def _sigmoid(x):
    return 1.0 / (1.0 + jnp.exp(-x))


def _silu(x):
    return x * _sigmoid(x)


def _cat(xs, axis):
    return xs[0] if len(xs) == 1 else jnp.concatenate(xs, axis=axis)


def _params(sem):
    return pltpu.CompilerParams(dimension_semantics=sem, vmem_limit_bytes=VMEM_LIMIT)


def _store_token_head(dst, val, heads):
    rows = val.shape[0]
    for h in range(heads):
        dst[pl.ds(h, rows, stride=heads), :] = val[:, h * PAIR:(h + 1) * PAIR]


def _proj_kernel(x_ref, nw_ref, w_ref, *refs, bounds, prompt, layer):
    x = x_ref[...]
    h = x * lax.rsqrt(jnp.mean(x * x, axis=-1, keepdims=True) + RMS_EPS) * nw_ref[...]
    hb = h.astype(BF16)
    c0, c1, c2, c3, c4, c5 = bounds
    mm = lambda lo, hi: _dot(hb, w_ref[:, lo:hi])
    if prompt:
        cols_ref, rg_ref, q_ref, kb_ref, vt_ref, dg_ref, kst_ref, vst_ref = refs[-8:]
    else:
        cols_ref, rg_ref, q_ref, k_ref, v_ref, dg_ref = refs
    cols_ref[...] = mm(0, c0)
    rg_ref[...] = mm(c0, c1)
    q = mm(c1, c2)
    k = mm(c2, c3)
    v = mm(c3, c4)
    dg_ref[...] = mm(c4, c5)
    if not prompt:
        q_ref[...] = q
        k_ref[...] = k
        v_ref[...] = v
        return
    q_ref[...] = (q * PROMPT_Q_SCALE).astype(BF16)
    kb_ref[...] = k.astype(BF16)
    vt_ref[0] = v.T.astype(BF16)
    heads = (c3 - c2) // PAIR
    if layer == 0:
        for slot in range(kst_ref.shape[0]):
            _store_token_head(kst_ref.at[slot], k, heads)
            _store_token_head(vst_ref.at[slot], v, heads)
    else:
        _store_token_head(kst_ref, k, heads)
        _store_token_head(vst_ref, v, heads)


N_PROJ_OUT = 8


def _proj_sample(x2d, norm_w, w_bf16, bounds, *, tm):
    m, d = x2d.shape
    row = lambda i: (i, 0)
    const = lambda i: (0, 0)
    widths = [hi - lo for lo, hi in zip((0,) + bounds[:-1], bounds)]
    return pl.pallas_call(
        functools.partial(_proj_kernel, bounds=bounds, prompt=False, layer=0),
        grid=(m // tm,),
        in_specs=[pl.BlockSpec((tm, d), row), pl.BlockSpec((1, d), const),
                  pl.BlockSpec(w_bf16.shape, const, pipeline_mode=pl.Buffered(1))],
        out_specs=[pl.BlockSpec((tm, w), row) for w in widths],
        out_shape=[jax.ShapeDtypeStruct((m, w), F32) for w in widths],
        compiler_params=_params(("arbitrary",)),
        name="proj_sample",
    )(x2d, norm_w.reshape(1, d), w_bf16)


def _head_sum(x):
    rows, width = x.shape
    lo = lax.broadcasted_iota(jnp.int32, (rows, PAIR), 1) < HEAD_DIM
    out = []
    for p in range(width // PAIR):
        xp = x[:, p * PAIR:(p + 1) * PAIR]
        s_lo = jnp.sum(jnp.where(lo, xp, 0.0), axis=-1, keepdims=True)
        s_hi = jnp.sum(jnp.where(lo, 0.0, xp), axis=-1, keepdims=True)
        out.append(jnp.where(lo, s_lo, s_hi))
    return jnp.concatenate(out, axis=-1)


def _split3(x):
    h0 = x.astype(BF16)
    r1 = x - h0.astype(F32)
    h1 = r1.astype(BF16)
    h2 = (r1 - h1.astype(F32)).astype(BF16)
    return h0, h1, h2


N_RWKV_IN = 13
N_ATTN_IN = 8


def _blockdiag(s_even, s_odd):
    zero = jnp.zeros((HEAD_DIM, HEAD_DIM), F32)
    return jnp.concatenate([jnp.concatenate([s_even, zero], axis=1), jnp.concatenate([zero, s_odd], axis=1)], axis=0)


def _rwkv_kernel(cols_ref, gate_ref, prev_ref, s0_ref, mu_ref, w0_ref, lora_ref, a0_ref, keyk_ref, keya_ref,
                 bonus_ref, lnw_ref, lnb_ref, *refs, rows, tlen, nsub, width, t_valid, carry, nslot):
    out_ref, sout_ref, s_scr, prev_scr = refs[-4:]
    tb = pl.program_id(1)
    nblk = pl.num_programs(1)
    npair = width // PAIR
    nseq = rows // tlen
    r2 = 2 * rows
    blk_rows = rows * nsub

    c = cols_ref[...]
    row_t = lax.broadcasted_iota(jnp.int32, c.shape, 0) % tlen if not carry else None
    if carry:
        @pl.when(tb == 0)
        def _():
            prev_scr[...] = prev_ref[0]
            for p in range(npair):
                s_scr[p] = _blockdiag(s0_ref[0, 2 * p], s0_ref[0, 2 * p + 1])

        row = lax.broadcasted_iota(jnp.int32, c.shape, 0)
        shifted = jnp.where(row == 0, prev_scr[...], pltpu.roll(c, 1, 0))
        prev_scr[...] = c[blk_rows - 1:blk_rows, :]
        state = {(0, p): s_scr[p] for p in range(npair)}
    else:
        shifted = jnp.where(row_t == 0, prev_ref[...], pltpu.roll(c, 1, 0))
        state = {(q, p): _blockdiag(s0_ref[q, 2 * p], s0_ref[q, 2 * p + 1])
                 for q in range(nsub * nseq) for p in range(npair)}

    xs = c + (shifted - c) * mu_ref[...]
    r = xs[:, 0:width]
    k = xs[:, width:2 * width]
    v = xs[:, 2 * width:3 * width]
    xwa = xs[:, 3 * width:3 * width + PAIR]
    lo_rank = lax.broadcasted_iota(jnp.int32, xwa.shape, 1) < HEAD_DIM
    xwa = jnp.where(lo_rank, jnp.tanh(xwa), xwa)
    lora = _dot(xwa.astype(BF16), lora_ref[...])
    z = -(w0_ref[...] + lora[:, 0:width])
    softplus = jnp.maximum(z, 0.0) + jnp.log(1.0 + jnp.exp(-jnp.abs(z)))
    lw = -jnp.exp(-softplus - 0.5)
    a = _sigmoid(a0_ref[...] + lora[:, width:2 * width])
    kk = k * keyk_ref[...]
    kk = kk / jnp.maximum(jnp.sqrt(_head_sum(kk * kk)), 1e-12)
    k2 = k * (1.0 + (a - 1.0) * keya_ref[...])
    av = -kk
    bv = kk * a
    if t_valid < tlen:
        live = lax.broadcasted_iota(jnp.int32, lw.shape, 0) % tlen < t_valid
        lw = jnp.where(live, lw, 0.0)
        av = jnp.where(live, av, 0.0)
        bv = jnp.where(live, bv, 0.0)
        k2s = jnp.where(live, k2, 0.0)
        vs = jnp.where(live, v, 0.0)
    else:
        k2s, vs = k2, v

    ri = lax.broadcasted_iota(jnp.int32, (r2, r2), 0)
    ci = lax.broadcasted_iota(jnp.int32, (r2, r2), 1)
    same = (ri // tlen) == (ci // tlen)
    strict = same & (ci < ri)
    incl = same & (ci <= ri)
    eye = (ri == ci).astype(F32)
    cum_lower = incl[0:rows, 0:rows]
    cum = jnp.concatenate([cum_lower, same[0:rows, 0:rows]], axis=0).astype(F32).astype(BF16)
    lo = lax.broadcasted_iota(jnp.int32, (rows, PAIR), 1) < HEAD_DIM
    nsteps = max(1, (tlen - 1).bit_length())

    def stack(xp):
        return jnp.concatenate([jnp.where(lo, xp, 0.0), jnp.where(lo, 0.0, xp)], axis=0)

    def seq_rows(x, q):
        if nseq == 1:
            return x
        return jnp.concatenate([x[q * tlen:(q + 1) * tlen], x[rows + q * tlen:rows + (q + 1) * tlen]], axis=0)

    def unseq(parts):
        if nseq == 1:
            return parts[0]
        return jnp.concatenate([x[0:tlen] for x in parts] + [x[tlen:2 * tlen] for x in parts], axis=0)

    probs = []
    for sub in range(nsub):
        sl = slice(sub * rows, (sub + 1) * rows)
        lwc = lw[sl]
        g0, g1, g2 = _split3(lwc)
        gg = _dot(cum, g0) + _dot(cum, g1) + _dot(cum, g2)
        g = gg[0:rows]
        g_end = gg[rows:r2]
        eg = jnp.exp(g)
        eng = jnp.exp(-g)
        egc = jnp.exp(g_end - g)
        at = av[sl] * jnp.exp(g - lwc)
        rt = r[sl] * eg
        bt = bv[sl] * eng
        kt = k2s[sl] * eng
        bh = bv[sl] * egc
        kh = k2s[sl] * egc
        e_end = jnp.exp(g_end)
        vc = vs[sl]
        for p in range(npair):
            ps = slice(p * PAIR, (p + 1) * PAIR)
            probs.append(dict(
                sub=sub, p=p,
                la=stack(at[:, ps]).astype(BF16), lr=stack(rt[:, ps]), lb=stack(bt[:, ps]).astype(BF16),
                lk=stack(kt[:, ps]).astype(BF16), lbh=stack(bh[:, ps]), lkh=stack(kh[:, ps]), lv=stack(vc[:, ps]),
                e_end=e_end[:, ps]))
    for pr in probs:
        pr["lrb"] = pr["lr"].astype(BF16)
        pr["lvb"] = pr["lv"].astype(BF16)
        sc = _dot_nt(jnp.concatenate([pr["la"], pr["lrb"]], axis=0),
                     jnp.concatenate([pr["lb"], pr["lk"]], axis=0))
        pr["pw"] = jnp.where(strict, sc[0:r2, 0:r2], 0.0)
        pr["dak"] = jnp.where(strict, sc[0:r2, r2:2 * r2], 0.0)
        pr["drb"] = jnp.where(incl, sc[r2:2 * r2, 0:r2], 0.0).astype(BF16)
        pr["drk"] = jnp.where(incl, sc[r2:2 * r2, r2:2 * r2], 0.0)
        pr["inv"] = eye + pr["pw"]
    for step in range(nsteps):
        for pr in probs:
            pb = pr["pw"].astype(BF16)
            if step == 0:
                if nsteps > 1:
                    pr["pw"] = _dot(pb, pb)
            elif step < nsteps - 1:
                px = _dot(pb, jnp.concatenate([pb, pr["inv"].astype(BF16)], axis=1))
                pr["pw"] = px[:, 0:r2]
                pr["inv"] = pr["inv"] + px[:, r2:2 * r2]
            else:
                pr["inv"] = pr["inv"] + _dot(pb, pr["inv"].astype(BF16))
    for pr in probs:
        pr["dv"] = _dot(jnp.concatenate([pr["dak"], pr["drk"]], axis=0).astype(BF16), pr["lvb"])
    for pr in probs:
        tw = _dot(pr["inv"].astype(BF16), jnp.concatenate([pr["la"], pr["dv"][0:r2].astype(BF16)], axis=1))
        pr["lw"] = tw[:, 0:PAIR]
        pr["lu0"] = tw[:, PAIR:2 * PAIR]

    y_chunks = []
    for sub in range(nsub):
        sub_probs = [pr for pr in probs if pr["sub"] == sub]
        for pr in sub_probs:
            pr["ws"] = []
            for q in range(nseq):
                key = (0 if carry else sub * nseq + q, pr["p"])
                lhs = jnp.concatenate([seq_rows(pr["lw"], q), seq_rows(pr["lr"], q)], axis=0).astype(BF16)
                pr["ws"].append(_dot_nt(lhs, state[key].astype(BF16)))
        for pr in sub_probs:
            lu = unseq([w[0:2 * tlen] for w in pr["ws"]]) + pr["lu0"]
            pr["lu"] = lu
            y2 = unseq([w[2 * tlen:4 * tlen] for w in pr["ws"]]) + _dot(pr["drb"], lu.astype(BF16)) + pr["dv"][r2:2 * r2]
            pr["y"] = y2[0:rows] + y2[rows:r2]
        for pr in sub_probs:
            for q in range(nseq):
                key = (0 if carry else sub * nseq + q, pr["p"])
                lhs = jnp.concatenate([seq_rows(pr["lu"], q), seq_rows(pr["lv"], q)], axis=0).astype(BF16)
                rhs = jnp.concatenate([seq_rows(pr["lbh"], q), seq_rows(pr["lkh"], q)], axis=0).astype(BF16)
                decay = pr["e_end"][q * tlen:q * tlen + 1, :]
                state[key] = state[key] * decay + _dot_tn(lhs, rhs)
        y_chunks.append(_cat([pr["y"] for pr in sub_probs], axis=-1))
    y = _cat(y_chunks, axis=0)

    inv_n = 1.0 / HEAD_DIM
    mean = _head_sum(y) * inv_n
    d = y - mean
    var = _head_sum(d * d) * inv_n
    yn = d * lax.rsqrt(var + GN_EPS) * lnw_ref[...] + lnb_ref[...]
    bonus = _head_sum(r * k2 * bonus_ref[...]) * v
    out_ref[...] = (yn + bonus) * _silu(gate_ref[...])

    views = [sout_ref.at[s] for s in range(sout_ref.shape[0])] if nslot else [sout_ref]

    def write_state(q, src):
        for p in range(npair):
            sp = src[(q, p)]
            for dst in views:
                dst[q, 2 * p] = sp[0:HEAD_DIM, 0:HEAD_DIM]
                dst[q, 2 * p + 1] = sp[HEAD_DIM:PAIR, HEAD_DIM:PAIR]

    if carry:
        for p in range(npair):
            s_scr[p] = state[(0, p)]

        @pl.when(tb == nblk - 1)
        def _():
            write_state(0, state)
    else:
        for q in range(nsub * nseq):
            write_state(q, state)


def _rwkv_prompt(cols, gate, prev, s0, lp, *, nseq_total, rows, nsub):
    m, ncols = cols.shape
    width = gate.shape[1]
    blk_rows = rows * nsub
    nblk = m // nseq_total // blk_rows
    hd = (width // HEAD_DIM, HEAD_DIM, HEAD_DIM)
    const = lambda b, t: (0, 0)
    vec = lambda n: pl.BlockSpec((1, n), const)
    blk = lambda b, t: (b * nblk + t, 0)
    state_spec = pl.BlockSpec((1,) + hd, lambda b, t: (b, 0, 0, 0))
    return pl.pallas_call(
        functools.partial(_rwkv_kernel, rows=rows, tlen=rows, nsub=nsub, width=width, t_valid=rows, carry=True,
                          nslot=0),
        grid=(nseq_total, nblk),
        in_specs=[pl.BlockSpec((blk_rows, ncols), blk), pl.BlockSpec((blk_rows, width), blk),
                  pl.BlockSpec((1, 1, ncols), lambda b, t: (b, 0, 0)), state_spec,
                  vec(ncols), vec(width), pl.BlockSpec((PAIR, 2 * width), const), vec(width), vec(width),
                  vec(width), vec(width), vec(width), vec(width)],
        out_specs=[pl.BlockSpec((blk_rows, width), blk), state_spec],
        out_shape=[jax.ShapeDtypeStruct((m, width), F32), jax.ShapeDtypeStruct((nseq_total,) + hd, F32)],
        scratch_shapes=[pltpu.VMEM((width // PAIR, PAIR, PAIR), F32), pltpu.VMEM((1, ncols), F32)],
        compiler_params=_params(("arbitrary", "arbitrary")),
        name="rwkv_prompt",
    )(cols, gate, prev.reshape(nseq_total, 1, ncols), s0, lp["mu"], lp["w0"], lp["lora"], lp["a0"], lp["key_k"],
      lp["key_a"], lp["bonus"], lp["lnx_w"], lp["lnx_b"])


def _rwkv_sample(cols, gate, prev, s0, lp, *, nseq_total, rows, tlen, nsub, t_valid, layer, depth, stacked=()):
    m, ncols = cols.shape
    width = gate.shape[1]
    blk_rows = rows * nsub
    const = lambda b, t: (0, 0)
    vec = lambda n: pl.BlockSpec((1, n), const)
    hd = (width // HEAD_DIM, HEAD_DIM, HEAD_DIM)
    blk = lambda b, t: (b, 0)
    sblk = blk_rows // tlen
    if layer == 0:
        nslot, extra_specs, aliases = depth, [], {}
        sout_spec = pl.BlockSpec((depth, sblk) + hd, lambda b, t: (0, b, 0, 0, 0))
    else:
        nslot, extra_specs, aliases = 0, [pl.BlockSpec(memory_space=pl.ANY)], {N_RWKV_IN: 1}
        sout_spec = pl.BlockSpec((None, sblk) + hd, lambda b, t: (layer, b, 0, 0, 0))
    return pl.pallas_call(
        functools.partial(_rwkv_kernel, rows=rows, tlen=tlen, nsub=nsub, width=width, t_valid=t_valid, carry=False,
                          nslot=nslot),
        grid=(m // blk_rows, 1),
        in_specs=[pl.BlockSpec((blk_rows, ncols), blk), pl.BlockSpec((blk_rows, width), blk),
                  pl.BlockSpec((blk_rows, ncols), blk), pl.BlockSpec((sblk,) + hd, lambda b, t: (b, 0, 0, 0)),
                  vec(ncols), vec(width), pl.BlockSpec((PAIR, 2 * width), const), vec(width), vec(width),
                  vec(width), vec(width), vec(width), vec(width)] + extra_specs,
        out_specs=[pl.BlockSpec((blk_rows, width), blk), sout_spec],
        out_shape=[jax.ShapeDtypeStruct((m, width), F32), jax.ShapeDtypeStruct((depth, nseq_total) + hd, F32)],
        input_output_aliases=aliases,
        scratch_shapes=[pltpu.VMEM((width // PAIR, PAIR, PAIR), F32), pltpu.VMEM((1, ncols), F32)],
        compiler_params=_params(("arbitrary", "arbitrary")),
        name="rwkv_sample",
    )(cols, gate, prev, s0, lp["mu"], lp["w0"], lp["lora"], lp["a0"], lp["key_k"],
      lp["key_a"], lp["bonus"], lp["lnx_w"], lp["lnx_b"], *stacked)


def _lambda(lam_ref, lam_init):
    lam = lam_ref[...]
    e1 = jnp.exp(jnp.sum(lam[0:1] * lam[1:2], axis=-1, keepdims=True))
    e2 = jnp.exp(jnp.sum(lam[2:3] * lam[3:4], axis=-1, keepdims=True))
    return e1 - e2 + lam_init


def _subln_gate(o, subw, gate, lam_init):
    o = o * lax.rsqrt(jnp.mean(o * o, axis=-1, keepdims=True) + SUBLN_EPS) * subw
    return o * (1.0 - lam_init) * _silu(gate)


def _attn_prompt_kernel(q_ref, k_ref, vt_ref, dg_ref, subw_ref, lam_ref, out_ref, vx_scr, *, tq, lam_init):
    seq = q_ref.shape[0]
    lo = lax.broadcasted_iota(jnp.int32, (tq, PAIR), 1) < HEAD_DIM
    ki = lax.broadcasted_iota(jnp.int32, (tq, 2 * tq), 0)
    qi = lax.broadcasted_iota(jnp.int32, (tq, 2 * tq), 1)
    causal = ki <= jnp.where(qi >= tq, qi - tq, qi)
    lam = _lambda(lam_ref, lam_init)
    vx_scr[0:PAIR, :] = vt_ref[...]
    vx_scr[PAIR:PAIR + BF16_ROWS, :] = jnp.ones((BF16_ROWS, seq), BF16)
    def scores(i):
        past, here = i * tq, (i + 1) * tq
        q = q_ref[past:here, :]
        zero = jnp.zeros_like(q)
        qs = jnp.concatenate([jnp.where(lo, q, zero), jnp.where(lo, zero, q)], axis=0)
        s_diag = jnp.where(causal, _dot_nt(k_ref[past:here, :], qs), -jnp.inf)
        s_past = _dot_nt(k_ref[0:past, :], qs) if i > 0 else None
        return s_diag, s_past

    nq = seq // tq
    ahead = scores(0)
    for i in range(nq):
        past, here = i * tq, (i + 1) * tq
        s_diag, s_past = ahead
        if i + 1 < nq:
            ahead = scores(i + 1)
        m = jnp.max(s_diag, axis=0, keepdims=True)
        if i > 0:
            m = jnp.maximum(m, jnp.max(s_past, axis=0, keepdims=True))
        acc = _dot(vx_scr[:, past:here], jnp.exp2(s_diag - m).astype(BF16))
        if i > 0:
            acc = acc + _dot(vx_scr[:, 0:past], jnp.exp2(s_past - m).astype(BF16))
        ot = acc[0:PAIR] / acc[PAIR:PAIR + 1]
        ot = ot[:, 0:tq] - lam * ot[:, tq:2 * tq]
        out_ref[past:here, :] = _subln_gate(ot.T, subw_ref[...], dg_ref[past:here, :], lam_init)


def _attn_prompt(q, kb, vt, dg, subw, lam_vecs, *, batch, seq, tq, lam_init):
    m, width = q.shape
    blk = pl.BlockSpec((seq, PAIR), lambda b, h: (b, h))
    const = lambda b, h: (0, 0)
    return pl.pallas_call(
        functools.partial(_attn_prompt_kernel, tq=tq, lam_init=lam_init),
        grid=(batch, width // PAIR),
        in_specs=[blk, blk, pl.BlockSpec((None, PAIR, seq), lambda b, h: (b, h, 0)), blk,
                  pl.BlockSpec((1, PAIR), const), pl.BlockSpec((4, HEAD_DIM), const)],
        out_specs=blk,
        out_shape=jax.ShapeDtypeStruct((m, width), F32),
        scratch_shapes=[pltpu.VMEM((PAIR + BF16_ROWS, seq), BF16)],
        compiler_params=_params(("arbitrary", "arbitrary")),
        name="attn_prompt",
    )(q, kb, vt, dg, subw, lam_vecs)


def _attn_sample_steps(pt_ref, q_ref, kn_ref, vn_ref, dg_ref, subw_ref, lam_ref, ck_ref, cv_ref, out_ref,
                        kbuf, vbuf, sem, *, b, i, first, last, npages, tpad, heads, lam_init, nseq):
    prow = kbuf.shape[2]

    def fetch(seq, slot):
        out = []
        for j in range(npages):
            row = pl.multiple_of(pt_ref[seq * npages + j] * prow, prow)
            out.append(pltpu.make_async_copy(ck_ref.at[pl.ds(row, prow), :], kbuf.at[slot, j], sem.at[slot, 0]))
            out.append(pltpu.make_async_copy(cv_ref.at[pl.ds(row, prow), :], vbuf.at[slot, j], sem.at[slot, 1]))
        return out

    def wait(slot):
        for j in range(npages):
            pltpu.make_async_copy(ck_ref.at[pl.ds(0, prow), :], kbuf.at[slot, j], sem.at[slot, 0]).wait()
            pltpu.make_async_copy(cv_ref.at[pl.ds(0, prow), :], vbuf.at[slot, j], sem.at[slot, 1]).wait()

    if first:
        @pl.when(b == 0)
        def _():
            for s in range(PAGE_SLOTS - 1):
                for c in fetch(min(s, nseq - 1), s):
                    c.start()

    slot = b % PAGE_SLOTS
    wait(slot)
    nrow = 2 * heads * tpad
    half = nrow // 2
    q = q_ref[i] * (1.0 / math.sqrt(HEAD_DIM))
    lo = lax.broadcasted_iota(jnp.int32, (tpad, PAIR), 1) < HEAD_DIM
    pieces = []
    for c in range(2):
        for h in range(heads):
            qh = q[:, h * PAIR:(h + 1) * PAIR]
            pieces.append(jnp.where(lo, qh, 0.0) if c == 0 else jnp.where(lo, 0.0, qh))
    qst = jnp.concatenate(pieces, axis=0).astype(BF16)

    def two(buf, j):
        return jnp.concatenate([buf[slot, j].astype(BF16), buf[slot, j + 1].astype(BF16)], axis=0)

    rhead = (lax.broadcasted_iota(jnp.int32, (nrow, 2 * prow), 0) // tpad) % heads
    chead = lax.broadcasted_iota(jnp.int32, (nrow, 2 * prow), 1) % heads
    own = rhead == chead
    scores = [jnp.where(own, _dot_nt(qst, two(kbuf, j)), -jnp.inf) for j in range(0, npages, 2)]

    def new_rows(ref):
        x = ref[i]
        xs = [x[:, h * PAIR:(h + 1) * PAIR] for h in range(heads)]
        xs.append(jnp.zeros((LANES - heads * tpad, PAIR), F32))
        return jnp.concatenate(xs, axis=0).astype(BF16)

    s_new = _dot_nt(qst, new_rows(kn_ref))
    rn = lax.broadcasted_iota(jnp.int32, (nrow, LANES), 0)
    cn = lax.broadcasted_iota(jnp.int32, (nrow, LANES), 1)
    visible = ((rn // tpad) % heads == cn // tpad) & (cn % tpad <= rn % tpad) & (cn < heads * tpad)
    s_new = jnp.where(visible, s_new, -jnp.inf)
    yield
    m = jnp.max(s_new, axis=-1, keepdims=True)
    for s in scores:
        m = jnp.maximum(m, jnp.max(s, axis=-1, keepdims=True))
    p_new = jnp.exp(s_new - m)
    l = jnp.sum(p_new, axis=-1, keepdims=True)
    probs = []
    for s in scores:
        p = jnp.exp(s - m)
        l = l + jnp.sum(p, axis=-1, keepdims=True)
        probs.append(p)
    inv_l = 1.0 / l
    w1 = inv_l[0:half]
    w2 = _lambda(lam_ref, lam_init) * inv_l[half:nrow]

    def mix(p):
        return (p[0:half] * w1 - p[half:nrow] * w2).astype(BF16)

    o = _dot(mix(p_new), new_rows(vn_ref))
    for idx, j in enumerate(range(0, npages, 2)):
        o = o + _dot(mix(probs[idx]), two(vbuf, j))
    dg = dg_ref[i]
    outs = [_subln_gate(o[h * tpad:(h + 1) * tpad], subw_ref[...], dg[:, h * PAIR:(h + 1) * PAIR], lam_init)
            for h in range(heads)]
    out_ref[i] = jnp.concatenate(outs, axis=-1)

    ahead = jnp.minimum(b + PAGE_SLOTS - 1, nseq - 1)
    for c in fetch(ahead, (b + PAGE_SLOTS - 1) % PAGE_SLOTS):
        c.start()

    if last:
        @pl.when(b == nseq - 1)
        def _():
            for s in range(1, PAGE_SLOTS):
                wait((nseq - 1 + s) % PAGE_SLOTS)


def _proj_attn_kernel(pt_ref, *refs, n_proj_in, proj_kw, attn_kw):
    n0, n1 = n_proj_in, n_proj_in + N_ATTN_IN
    proj_out, attn_out, attn_scr = refs[n1:n1 + N_PROJ_OUT], refs[n1 + N_PROJ_OUT], refs[n1 + N_PROJ_OUT + 1:]
    nper = attn_out.shape[0]
    step = pl.program_id(0)
    attn = [_attn_sample_steps(pt_ref, *refs[n0:n1], attn_out, *attn_scr, b=step * nper + i, i=i, first=i == 0,
                               last=i == nper - 1, **attn_kw) for i in range(nper)]
    next(attn[0])
    _proj_kernel(*refs[0:n0], *proj_out, **proj_kw)
    for steps in attn:
        for _ in steps:
            pass


def _proj_prompt_attn_sample(x2d, norm_w, w_bf16, bounds, q, kn, vn, dg, subw, lam_vecs, cache_k, cache_v, page_table,
                             *, tm, seq, layer, depth, stacked, lam_init):
    m, d = x2d.shape
    c0, c1, c2, c3, c4, c5 = bounds
    nstep = m // tm
    nt = seq // tm
    pheads = (c3 - c2) // PAIR
    batch, tpad, dwidth = q.shape
    nper = batch // nstep
    assert nper * nstep == batch
    npages = page_table.shape[1]
    n_phys, page, heads = cache_k.shape[1], cache_k.shape[2], cache_k.shape[3]
    prow = page * heads
    ck = cache_k.reshape(-1, PAIR)
    cv = cache_v.reshape(-1, PAIR)
    pt = page_table.reshape(-1).astype(jnp.int32) + layer * n_phys

    row = lambda i, pt: (i, 0)
    const = lambda i, pt: (0, 0)
    any_spec = pl.BlockSpec(memory_space=pl.ANY)
    tok = pl.BlockSpec((nper, tpad, dwidth), lambda i, pt: (i, 0, 0))
    wide = lambda w, dt: (pl.BlockSpec((tm, w), row), jax.ShapeDtypeStruct((m, w), dt))
    proj_in = [pl.BlockSpec((tm, d), row), pl.BlockSpec((1, d), const),
               pl.BlockSpec(w_bf16.shape, const, pipeline_mode=pl.Buffered(1))]
    if layer == 0:
        st_spec = pl.BlockSpec((depth, tm * pheads, PAIR), lambda i, pt: (0, i, 0))
        aliases = {}
    else:
        st_spec = pl.BlockSpec((None, tm * pheads, PAIR), lambda i, pt: (layer, i, 0))
        proj_in += [any_spec, any_spec]
        aliases = {4: 6, 5: 7}
    outs = [wide(c0, F32), wide(c1 - c0, F32), wide(c2 - c1, BF16), wide(c3 - c2, BF16),
            (pl.BlockSpec((1, c4 - c3, tm), lambda i, pt: (i // nt, 0, i % nt)),
             jax.ShapeDtypeStruct((m // seq, c4 - c3, seq), BF16)),
            wide(c5 - c4, F32)]
    outs += [(st_spec, jax.ShapeDtypeStruct((depth, m * pheads, PAIR), F32))] * 2
    assert len(outs) == N_PROJ_OUT
    outs.append((tok, jax.ShapeDtypeStruct((batch, tpad, dwidth), F32)))
    grid_spec = pltpu.PrefetchScalarGridSpec(
        num_scalar_prefetch=1,
        grid=(nstep,),
        in_specs=proj_in + [tok, tok, tok, tok, pl.BlockSpec((1, PAIR), const), pl.BlockSpec((4, HEAD_DIM), const),
                            any_spec, any_spec],
        out_specs=[o[0] for o in outs],
        scratch_shapes=[pltpu.VMEM((PAGE_SLOTS, npages, prow, PAIR), F32),
                        pltpu.VMEM((PAGE_SLOTS, npages, prow, PAIR), F32),
                        pltpu.SemaphoreType.DMA((PAGE_SLOTS, 2))],
    )
    proj_kw = dict(bounds=bounds, prompt=True, layer=layer)
    attn_kw = dict(npages=npages, tpad=tpad, heads=heads, lam_init=lam_init, nseq=batch)
    return pl.pallas_call(
        functools.partial(_proj_attn_kernel, n_proj_in=len(proj_in), proj_kw=proj_kw, attn_kw=attn_kw),
        grid_spec=grid_spec,
        out_shape=[o[1] for o in outs],
        input_output_aliases=aliases,
        compiler_params=pltpu.CompilerParams(dimension_semantics=("arbitrary",), vmem_limit_bytes=VMEM_LIMIT_FUSED),
        name="proj_prompt_attn_sample",
    )(pt, x2d, norm_w.reshape(1, d), w_bf16, *stacked, q, kn, vn, dg, subw, lam_vecs, ck, cv)


def _outproj_kernel(x_ref, r_ref, d_ref, w_ref, fw_ref, o_ref, *, final):
    half = r_ref.shape[1]
    y = x_ref[...] + _dot(r_ref[...].astype(BF16), w_ref[0:half, :]) + _dot(d_ref[...].astype(BF16), w_ref[half:, :])
    if final:
        y = y * lax.rsqrt(jnp.mean(y * y, axis=-1, keepdims=True) + RMS_EPS) * fw_ref[...]
    o_ref[...] = y


def _outproj(x2d, rw, df, w_bf16, final_w, *, final, tm):
    m, d = x2d.shape
    half = rw.shape[1]
    row = lambda i: (i, 0)
    const = lambda i: (0, 0)
    return pl.pallas_call(
        functools.partial(_outproj_kernel, final=final),
        grid=(m // tm,),
        in_specs=[pl.BlockSpec((tm, d), row), pl.BlockSpec((tm, half), row), pl.BlockSpec((tm, half), row),
                  pl.BlockSpec(w_bf16.shape, const), pl.BlockSpec((1, d), const)],
        out_specs=pl.BlockSpec((tm, d), row),
        out_shape=jax.ShapeDtypeStruct((m, d), F32),
        compiler_params=_params(("arbitrary",)),
        name="outproj",
    )(x2d, rw, df, w_bf16, final_w.reshape(1, d))


def kernel(x_prompt, x_sample, cache_k, cache_v, page_table, state_rwkv, state_shift, norm_w, w_in, shift_mu, decay_w0, decay_lora_b, iclr_a0, iclr_lora_b, key_k, key_a, bonus_r_k, lnx_w, lnx_b, lam_q1, lam_k1, lam_q2, lam_k2, subln_w, w_out, final_norm_w):
    depth = w_in.shape[0]
    bp, seq, d = x_prompt.shape
    bs, tdec, _ = x_sample.shape
    width = decay_w0.shape[-1]
    ncols = shift_mu.shape[-1]
    rank_w, rank_a = decay_lora_b.shape[1], iclr_lora_b.shape[1]
    heads_r = width // HEAD_DIM
    heads_d = cache_k.shape[3]
    dwidth = heads_d * cache_v.shape[4]
    dqk = heads_d * cache_k.shape[4]
    assert rank_w == HEAD_DIM and rank_a == HEAD_DIM and ncols == 3 * width + PAIR
    assert cache_k.shape[4] == PAIR and cache_v.shape[4] == PAIR
    c0 = ncols
    c1 = c0 + width
    c2 = c1 + dqk
    c3 = c2 + dqk
    c4 = c3 + dwidth
    c5 = c4 + dwidth
    bounds = (c0, c1, c2, c3, c4, c5)
    assert w_in.shape[2] == c5

    tpad = SUBLANES
    chunk = 64
    tile = 256
    assert tdec <= tpad
    xp = x_prompt.reshape(bp * seq, d)
    xs = jnp.pad(x_sample, ((0, 0), (0, tpad - tdec), (0, 0))).reshape(bs * tpad, d)

    kst = vst = sst = None
    outs = {n: [] for n in ("sp", "hp", "ks", "vs", "hs")}
    for l in range(depth):
        lam_init = 0.8 - 0.6 * math.exp(-0.3 * l)
        w_l = w_in[l].astype(BF16)
        wo_l = w_out[l].astype(BF16)
        zeros = jnp.zeros((rank_w, width), F32)
        lora = jnp.concatenate([jnp.concatenate([decay_lora_b[l], zeros], axis=1),
                                jnp.concatenate([zeros, iclr_lora_b[l]], axis=1)], axis=0).astype(BF16)
        lp = dict(mu=shift_mu[l].reshape(1, ncols), w0=decay_w0[l].reshape(1, width), lora=lora,
                  a0=iclr_a0[l].reshape(1, width), key_k=key_k[l].reshape(1, width), key_a=key_a[l].reshape(1, width),
                  bonus=bonus_r_k[l].reshape(1, width), lnx_w=lnx_w[l].reshape(1, width),
                  lnx_b=lnx_b[l].reshape(1, width))
        lam_vecs = jnp.stack([lam_q1[l], lam_k1[l], lam_q2[l], lam_k2[l]], axis=0)
        subw = subln_w[l].reshape(1, PAIR)
        last = l == depth - 1

        cols_s, rg_s, q_s, k_s, v_s, dg_s = _proj_sample(xs, norm_w[l], w_l, bounds, tm=tile)
        r3 = lambda t: t.reshape(bs, tpad, -1)
        cols, rg, q, kb, vt, dg, kst, vst, df_s = _proj_prompt_attn_sample(
            xp, norm_w[l], w_l, bounds, r3(q_s), r3(k_s), r3(v_s), r3(dg_s), subw, lam_vecs, cache_k, cache_v,
            page_table, tm=tile, seq=seq, layer=l, depth=depth, stacked=() if l == 0 else (kst, vst),
            lam_init=lam_init)

        rw, s_new = _rwkv_prompt(cols, rg, jnp.zeros((bp, ncols), F32),
                                 jnp.zeros((bp, heads_r, HEAD_DIM, HEAD_DIM), F32), lp,
                                 nseq_total=bp, rows=chunk, nsub=4)
        df = _attn_prompt(q, kb, vt, dg, subw, lam_vecs, batch=bp, seq=seq, tq=tile, lam_init=lam_init)
        xp = _outproj(xp, rw, df, wo_l, final_norm_w, final=last, tm=2 * tile)
        outs["sp"].append(s_new)
        outs["hp"].append(cols.reshape(bp, seq, ncols)[:, -1, :])

        prev = jnp.pad(state_shift[l][:, None, :], ((0, 0), (0, tpad - 1), (0, 0))).reshape(bs * tpad, ncols)
        rw_s, sst = _rwkv_sample(cols_s, rg_s, prev, state_rwkv[l], lp, nseq_total=bs, rows=chunk, tlen=tpad, nsub=1,
                                 t_valid=tdec, layer=l, depth=depth, stacked=() if l == 0 else (sst,))
        xs = _outproj(xs, rw_s, df_s.reshape(bs * tpad, dwidth), wo_l, final_norm_w, final=last, tm=tile)
        outs["ks"].append(r3(k_s)[:, :tdec].reshape(bs, tdec, heads_d, PAIR))
        outs["vs"].append(r3(v_s)[:, :tdec].reshape(bs, tdec, heads_d, PAIR))
        outs["hs"].append(r3(cols_s)[:, tdec - 1, :])

    y_prompt = xp.reshape(bp, seq, d)
    y_sample = xs.reshape(bs, tpad, d)[:, :tdec]
    st = lambda n: jnp.stack(outs[n], axis=0)
    k_prompt = kst.reshape(depth, bp, seq, heads_d, PAIR)
    v_prompt = vst.reshape(depth, bp, seq, heads_d, PAIR)
    return (y_prompt, y_sample, k_prompt, v_prompt, st("sp"), st("hp"), st("ks"), st("vs"), sst, st("hs"))
```

```python
import functools
import math

import jax
import jax.numpy as jnp
from jax import lax
from jax.experimental import pallas as pl
from jax.experimental.pallas import tpu as pltpu

F32 = jnp.float32
BF16 = jnp.bfloat16

RMS_EPS = 1e-6
SUBLN_EPS = 1e-5
GN_EPS = 64e-5

LANES = 128
SUBLANES = 8
HEAD_DIM = 64
PAIR = 2 * HEAD_DIM
VMEM_LIMIT = 48 * 1024 * 1024
VMEM_LIMIT_FUSED = 58 * 1024 * 1024
BF16_ROWS = 16
PAGE_SLOTS = 3
PROMPT_Q_SCALE = math.log2(math.e) / math.sqrt(HEAD_DIM)

_NT = (((1,), (1,)), ((), ()))
_TN = (((0,), (0,)), ((), ()))


def _dot(a, b):
    return jnp.dot(a, b, preferred_element_type=F32)


def _dot_nt(a, b):
    return lax.dot_general(a, b, _NT, preferred_element_type=F32)


def _dot_tn(a, b):
    return lax.dot_general(a, b, _TN, preferred_element_type=F32)


def _sigmoid(x):
    return 1.0 / (1.0 + jnp.exp(-x))


def _silu(x):
    return x * _sigmoid(x)


def _cat(xs, axis):
    return xs[0] if len(xs) == 1 else jnp.concatenate(xs, axis=axis)


def _params(sem):
    return pltpu.CompilerParams(dimension_semantics=sem, vmem_limit_bytes=VMEM_LIMIT)


def _store_token_head(dst, val, heads):
    rows = val.shape[0]
    for h in range(heads):
        dst[pl.ds(h, rows, stride=heads), :] = val[:, h * PAIR:(h + 1) * PAIR]


def _proj_kernel(x_ref, nw_ref, w_ref, *refs, bounds, prompt, layer):
    x = x_ref[...]
    h = x * lax.rsqrt(jnp.mean(x * x, axis=-1, keepdims=True) + RMS_EPS) * nw_ref[...]
    hb = h.astype(BF16)
    c0, c1, c2, c3, c4, c5 = bounds
    mm = lambda lo, hi: _dot(hb, w_ref[:, lo:hi])
    if prompt:
        cols_ref, rg_ref, q_ref, kb_ref, vt_ref, dg_ref, kst_ref, vst_ref = refs[-8:]
    else:
        cols_ref, rg_ref, q_ref, k_ref, v_ref, dg_ref = refs
    cols_ref[...] = mm(0, c0)
    rg_ref[...] = mm(c0, c1)
    q = mm(c1, c2)
    k = mm(c2, c3)
    v = mm(c3, c4)
    dg_ref[...] = mm(c4, c5)
    if not prompt:
        q_ref[...] = q
        k_ref[...] = k
        v_ref[...] = v
        return
    q_ref[...] = (q * PROMPT_Q_SCALE).astype(BF16)
    kb_ref[...] = k.astype(BF16)
    vt_ref[0] = v.T.astype(BF16)
    heads = (c3 - c2) // PAIR
    if layer == 0:
        for slot in range(kst_ref.shape[0]):
            _store_token_head(kst_ref.at[slot], k, heads)
            _store_token_head(vst_ref.at[slot], v, heads)
    else:
        _store_token_head(kst_ref, k, heads)
        _store_token_head(vst_ref, v, heads)


N_PROJ_OUT = 8


def _proj_sample(x2d, norm_w, w_bf16, bounds, *, tm):
    m, d = x2d.shape
    row = lambda i: (i, 0)
    const = lambda i: (0, 0)
    widths = [hi - lo for lo, hi in zip((0,) + bounds[:-1], bounds)]
    return pl.pallas_call(
        functools.partial(_proj_kernel, bounds=bounds, prompt=False, layer=0),
        grid=(m // tm,),
        in_specs=[pl.BlockSpec((tm, d), row), pl.BlockSpec((1, d), const),
                  pl.BlockSpec(w_bf16.shape, const, pipeline_mode=pl.Buffered(1))],
        out_specs=[pl.BlockSpec((tm, w), row) for w in widths],
        out_shape=[jax.ShapeDtypeStruct((m, w), F32) for w in widths],
        compiler_params=_params(("arbitrary",)),
        name="proj_sample",
    )(x2d, norm_w.reshape(1, d), w_bf16)


def _head_sum(x):
    rows, width = x.shape
    lo = lax.broadcasted_iota(jnp.int32, (rows, PAIR), 1) < HEAD_DIM
    out = []
    for p in range(width // PAIR):
        xp = x[:, p * PAIR:(p + 1) * PAIR]
        s_lo = jnp.sum(jnp.where(lo, xp, 0.0), axis=-1, keepdims=True)
        s_hi = jnp.sum(jnp.where(lo, 0.0, xp), axis=-1, keepdims=True)
        out.append(jnp.where(lo, s_lo, s_hi))
    return jnp.concatenate(out, axis=-1)


def _split3(x):
    h0 = x.astype(BF16)
    r1 = x - h0.astype(F32)
    h1 = r1.astype(BF16)
    h2 = (r1 - h1.astype(F32)).astype(BF16)
    return h0, h1, h2


N_RWKV_IN = 13
N_ATTN_IN = 8


def _blockdiag(s_even, s_odd):
    zero = jnp.zeros((HEAD_DIM, HEAD_DIM), F32)
    return jnp.concatenate([jnp.concatenate([s_even, zero], axis=1), jnp.concatenate([zero, s_odd], axis=1)], axis=0)


def _rwkv_kernel(cols_ref, gate_ref, prev_ref, s0_ref, mu_ref, w0_ref, lora_ref, a0_ref, keyk_ref, keya_ref,
                 bonus_ref, lnw_ref, lnb_ref, *refs, rows, tlen, nsub, width, t_valid, carry, nslot):
    out_ref, sout_ref, s_scr, prev_scr = refs[-4:]
    tb = pl.program_id(1)
    nblk = pl.num_programs(1)
    npair = width // PAIR
    nseq = rows // tlen
    r2 = 2 * rows
    blk_rows = rows * nsub

    c = cols_ref[...]
    row_t = lax.broadcasted_iota(jnp.int32, c.shape, 0) % tlen if not carry else None
    if carry:
        @pl.when(tb == 0)
        def _():
            prev_scr[...] = prev_ref[0]
            for p in range(npair):
                s_scr[p] = _blockdiag(s0_ref[0, 2 * p], s0_ref[0, 2 * p + 1])

        row = lax.broadcasted_iota(jnp.int32, c.shape, 0)
        shifted = jnp.where(row == 0, prev_scr[...], pltpu.roll(c, 1, 0))
        prev_scr[...] = c[blk_rows - 1:blk_rows, :]
        state = {(0, p): s_scr[p] for p in range(npair)}
    else:
        shifted = jnp.where(row_t == 0, prev_ref[...], pltpu.roll(c, 1, 0))
        state = {(q, p): _blockdiag(s0_ref[q, 2 * p], s0_ref[q, 2 * p + 1])
                 for q in range(nsub * nseq) for p in range(npair)}

    xs = c + (shifted - c) * mu_ref[...]
    r = xs[:, 0:width]
    k = xs[:, width:2 * width]
    v = xs[:, 2 * width:3 * width]
    xwa = xs[:, 3 * width:3 * width + PAIR]
    lo_rank = lax.broadcasted_iota(jnp.int32, xwa.shape, 1) < HEAD_DIM
    xwa = jnp.where(lo_rank, jnp.tanh(xwa), xwa)
    lora = _dot(xwa.astype(BF16), lora_ref[...])
    z = -(w0_ref[...] + lora[:, 0:width])
    softplus = jnp.maximum(z, 0.0) + jnp.log(1.0 + jnp.exp(-jnp.abs(z)))
    lw = -jnp.exp(-softplus - 0.5)
    a = _sigmoid(a0_ref[...] + lora[:, width:2 * width])
    kk = k * keyk_ref[...]
    kk = kk / jnp.maximum(jnp.sqrt(_head_sum(kk * kk)), 1e-12)
    k2 = k * (1.0 + (a - 1.0) * keya_ref[...])
    av = -kk
    bv = kk * a
    if t_valid < tlen:
        live = lax.broadcasted_iota(jnp.int32, lw.shape, 0) % tlen < t_valid
        lw = jnp.where(live, lw, 0.0)
        av = jnp.where(live, av, 0.0)
        bv = jnp.where(live, bv, 0.0)
        k2s = jnp.where(live, k2, 0.0)
        vs = jnp.where(live, v, 0.0)
    else:
        k2s, vs = k2, v

    ri = lax.broadcasted_iota(jnp.int32, (r2, r2), 0)
    ci = lax.broadcasted_iota(jnp.int32, (r2, r2), 1)
    same = (ri // tlen) == (ci // tlen)
    strict = same & (ci < ri)
    incl = same & (ci <= ri)
    eye = (ri == ci).astype(F32)
    cum_lower = incl[0:rows, 0:rows]
    cum = jnp.concatenate([cum_lower, same[0:rows, 0:rows]], axis=0).astype(F32).astype(BF16)
    lo = lax.broadcasted_iota(jnp.int32, (rows, PAIR), 1) < HEAD_DIM
    nsteps = max(1, (tlen - 1).bit_length())

    def stack(xp):
        return jnp.concatenate([jnp.where(lo, xp, 0.0), jnp.where(lo, 0.0, xp)], axis=0)

    def seq_rows(x, q):
        if nseq == 1:
            return x
        return jnp.concatenate([x[q * tlen:(q + 1) * tlen], x[rows + q * tlen:rows + (q + 1) * tlen]], axis=0)

    def unseq(parts):
        if nseq == 1:
            return parts[0]
        return jnp.concatenate([x[0:tlen] for x in parts] + [x[tlen:2 * tlen] for x in parts], axis=0)

    probs = []
    for sub in range(nsub):
        sl = slice(sub * rows, (sub + 1) * rows)
        lwc = lw[sl]
        g0, g1, g2 = _split3(lwc)
        gg = _dot(cum, g0) + _dot(cum, g1) + _dot(cum, g2)
        g = gg[0:rows]
        g_end = gg[rows:r2]
        eg = jnp.exp(g)
        eng = jnp.exp(-g)
        egc = jnp.exp(g_end - g)
        at = av[sl] * jnp.exp(g - lwc)
        rt = r[sl] * eg
        bt = bv[sl] * eng
        kt = k2s[sl] * eng
        bh = bv[sl] * egc
        kh = k2s[sl] * egc
        e_end = jnp.exp(g_end)
        vc = vs[sl]
        for p in range(npair):
            ps = slice(p * PAIR, (p + 1) * PAIR)
            probs.append(dict(
                sub=sub, p=p,
                la=stack(at[:, ps]).astype(BF16), lr=stack(rt[:, ps]), lb=stack(bt[:, ps]).astype(BF16),
                lk=stack(kt[:, ps]).astype(BF16), lbh=stack(bh[:, ps]), lkh=stack(kh[:, ps]), lv=stack(vc[:, ps]),
                e_end=e_end[:, ps]))
    for pr in probs:
        pr["lrb"] = pr["lr"].astype(BF16)
        pr["lvb"] = pr["lv"].astype(BF16)
        sc = _dot_nt(jnp.concatenate([pr["la"], pr["lrb"]], axis=0),
                     jnp.concatenate([pr["lb"], pr["lk"]], axis=0))
        pr["pw"] = jnp.where(strict, sc[0:r2, 0:r2], 0.0)
        pr["dak"] = jnp.where(strict, sc[0:r2, r2:2 * r2], 0.0)
        pr["drb"] = jnp.where(incl, sc[r2:2 * r2, 0:r2], 0.0).astype(BF16)
        pr["drk"] = jnp.where(incl, sc[r2:2 * r2, r2:2 * r2], 0.0)
        pr["inv"] = eye + pr["pw"]
    for step in range(nsteps):
        for pr in probs:
            pb = pr["pw"].astype(BF16)
            if step == 0:
                if nsteps > 1:
                    pr["pw"] = _dot(pb, pb)
            elif step < nsteps - 1:
                px = _dot(pb, jnp.concatenate([pb, pr["inv"].astype(BF16)], axis=1))
                pr["pw"] = px[:, 0:r2]
                pr["inv"] = pr["inv"] + px[:, r2:2 * r2]
            else:
                pr["inv"] = pr["inv"] + _dot(pb, pr["inv"].astype(BF16))
    for pr in probs:
        pr["dv"] = _dot(jnp.concatenate([pr["dak"], pr["drk"]], axis=0).astype(BF16), pr["lvb"])
    for pr in probs:
        tw = _dot(pr["inv"].astype(BF16), jnp.concatenate([pr["la"], pr["dv"][0:r2].astype(BF16)], axis=1))
        pr["lw"] = tw[:, 0:PAIR]
        pr["lu0"] = tw[:, PAIR:2 * PAIR]

    y_chunks = []
    for sub in range(nsub):
        sub_probs = [pr for pr in probs if pr["sub"] == sub]
        for pr in sub_probs:
            pr["ws"] = []
            for q in range(nseq):
                key = (0 if carry else sub * nseq + q, pr["p"])
                lhs = jnp.concatenate([seq_rows(pr["lw"], q), seq_rows(pr["lr"], q)], axis=0).astype(BF16)
                pr["ws"].append(_dot_nt(lhs, state[key].astype(BF16)))
        for pr in sub_probs:
            lu = unseq([w[0:2 * tlen] for w in pr["ws"]]) + pr["lu0"]
            pr["lu"] = lu
            y2 = unseq([w[2 * tlen:4 * tlen] for w in pr["ws"]]) + _dot(pr["drb"], lu.astype(BF16)) + pr["dv"][r2:2 * r2]
            pr["y"] = y2[0:rows] + y2[rows:r2]
        for pr in sub_probs:
            for q in range(nseq):
                key = (0 if carry else sub * nseq + q, pr["p"])
                lhs = jnp.concatenate([seq_rows(pr["lu"], q), seq_rows(pr["lv"], q)], axis=0).astype(BF16)
                rhs = jnp.concatenate([seq_rows(pr["lbh"], q), seq_rows(pr["lkh"], q)], axis=0).astype(BF16)
                decay = pr["e_end"][q * tlen:q * tlen + 1, :]
                state[key] = state[key] * decay + _dot_tn(lhs, rhs)
        y_chunks.append(_cat([pr["y"] for pr in sub_probs], axis=-1))
    y = _cat(y_chunks, axis=0)

    inv_n = 1.0 / HEAD_DIM
    mean = _head_sum(y) * inv_n
    d = y - mean
    var = _head_sum(d * d) * inv_n
    yn = d * lax.rsqrt(var + GN_EPS) * lnw_ref[...] + lnb_ref[...]
    bonus = _head_sum(r * k2 * bonus_ref[...]) * v
    out_ref[...] = (yn + bonus) * _silu(gate_ref[...])

    views = [sout_ref.at[s] for s in range(sout_ref.shape[0])] if nslot else [sout_ref]

    def write_state(q, src):
        for p in range(npair):
            sp = src[(q, p)]
            for dst in views:
                dst[q, 2 * p] = sp[0:HEAD_DIM, 0:HEAD_DIM]
                dst[q, 2 * p + 1] = sp[HEAD_DIM:PAIR, HEAD_DIM:PAIR]

    if carry:
        for p in range(npair):
            s_scr[p] = state[(0, p)]

        @pl.when(tb == nblk - 1)
        def _():
            write_state(0, state)
    else:
        for q in range(nsub * nseq):
            write_state(q, state)


def _rwkv_prompt(cols, gate, prev, s0, lp, *, nseq_total, rows, nsub):
    m, ncols = cols.shape
    width = gate.shape[1]
    blk_rows = rows * nsub
    nblk = m // nseq_total // blk_rows
    hd = (width // HEAD_DIM, HEAD_DIM, HEAD_DIM)
    const = lambda b, t: (0, 0)
    vec = lambda n: pl.BlockSpec((1, n), const)
    blk = lambda b, t: (b * nblk + t, 0)
    state_spec = pl.BlockSpec((1,) + hd, lambda b, t: (b, 0, 0, 0))
    return pl.pallas_call(
        functools.partial(_rwkv_kernel, rows=rows, tlen=rows, nsub=nsub, width=width, t_valid=rows, carry=True,
                          nslot=0),
        grid=(nseq_total, nblk),
        in_specs=[pl.BlockSpec((blk_rows, ncols), blk), pl.BlockSpec((blk_rows, width), blk),
                  pl.BlockSpec((1, 1, ncols), lambda b, t: (b, 0, 0)), state_spec,
                  vec(ncols), vec(width), pl.BlockSpec((PAIR, 2 * width), const), vec(width), vec(width),
                  vec(width), vec(width), vec(width), vec(width)],
        out_specs=[pl.BlockSpec((blk_rows, width), blk), state_spec],
        out_shape=[jax.ShapeDtypeStruct((m, width), F32), jax.ShapeDtypeStruct((nseq_total,) + hd, F32)],
        scratch_shapes=[pltpu.VMEM((width // PAIR, PAIR, PAIR), F32), pltpu.VMEM((1, ncols), F32)],
        compiler_params=_params(("arbitrary", "arbitrary")),
        name="rwkv_prompt",
    )(cols, gate, prev.reshape(nseq_total, 1, ncols), s0, lp["mu"], lp["w0"], lp["lora"], lp["a0"], lp["key_k"],
      lp["key_a"], lp["bonus"], lp["lnx_w"], lp["lnx_b"])


def _rwkv_sample(cols, gate, prev, s0, lp, *, nseq_total, rows, tlen, nsub, t_valid, layer, depth, stacked=()):
    m, ncols = cols.shape
    width = gate.shape[1]
    blk_rows = rows * nsub
    const = lambda b, t: (0, 0)
    vec = lambda n: pl.BlockSpec((1, n), const)
    hd = (width // HEAD_DIM, HEAD_DIM, HEAD_DIM)
    blk = lambda b, t: (b, 0)
    sblk = blk_rows // tlen
    if layer == 0:
        nslot, extra_specs, aliases = depth, [], {}
        sout_spec = pl.BlockSpec((depth, sblk) + hd, lambda b, t: (0, b, 0, 0, 0))
    else:
        nslot, extra_specs, aliases = 0, [pl.BlockSpec(memory_space=pl.ANY)], {N_RWKV_IN: 1}
        sout_spec = pl.BlockSpec((None, sblk) + hd, lambda b, t: (layer, b, 0, 0, 0))
    return pl.pallas_call(
        functools.partial(_rwkv_kernel, rows=rows, tlen=tlen, nsub=nsub, width=width, t_valid=t_valid, carry=False,
                          nslot=nslot),
        grid=(m // blk_rows, 1),
        in_specs=[pl.BlockSpec((blk_rows, ncols), blk), pl.BlockSpec((blk_rows, width), blk),
                  pl.BlockSpec((blk_rows, ncols), blk), pl.BlockSpec((sblk,) + hd, lambda b, t: (b, 0, 0, 0)),
                  vec(ncols), vec(width), pl.BlockSpec((PAIR, 2 * width), const), vec(width), vec(width),
                  vec(width), vec(width), vec(width), vec(width)] + extra_specs,
        out_specs=[pl.BlockSpec((blk_rows, width), blk), sout_spec],
        out_shape=[jax.ShapeDtypeStruct((m, width), F32), jax.ShapeDtypeStruct((depth, nseq_total) + hd, F32)],
        input_output_aliases=aliases,
        scratch_shapes=[pltpu.VMEM((width // PAIR, PAIR, PAIR), F32), pltpu.VMEM((1, ncols), F32)],
        compiler_params=_params(("arbitrary", "arbitrary")),
        name="rwkv_sample",
    )(cols, gate, prev, s0, lp["mu"], lp["w0"], lp["lora"], lp["a0"], lp["key_k"],
      lp["key_a"], lp["bonus"], lp["lnx_w"], lp["lnx_b"], *stacked)


def _lambda(lam_ref, lam_init):
    lam = lam_ref[...]
    e1 = jnp.exp(jnp.sum(lam[0:1] * lam[1:2], axis=-1, keepdims=True))
    e2 = jnp.exp(jnp.sum(lam[2:3] * lam[3:4], axis=-1, keepdims=True))
    return e1 - e2 + lam_init


def _subln_gate(o, subw, gate, lam_init):
    o = o * lax.rsqrt(jnp.mean(o * o, axis=-1, keepdims=True) + SUBLN_EPS) * subw
    return o * (1.0 - lam_init) * _silu(gate)


def _attn_prompt_kernel(q_ref, k_ref, vt_ref, dg_ref, subw_ref, lam_ref, out_ref, vx_scr, *, tq, lam_init):
    seq = q_ref.shape[0]
    lo = lax.broadcasted_iota(jnp.int32, (tq, PAIR), 1) < HEAD_DIM
    ki = lax.broadcasted_iota(jnp.int32, (tq, 2 * tq), 0)
    qi = lax.broadcasted_iota(jnp.int32, (tq, 2 * tq), 1)
    causal = ki <= jnp.where(qi >= tq, qi - tq, qi)
    lam = _lambda(lam_ref, lam_init)
    vx_scr[0:PAIR, :] = vt_ref[...]
    vx_scr[PAIR:PAIR + BF16_ROWS, :] = jnp.ones((BF16_ROWS, seq), BF16)
    def scores(i):
        past, here = i * tq, (i + 1) * tq
        q = q_ref[past:here, :]
        zero = jnp.zeros_like(q)
        qs = jnp.concatenate([jnp.where(lo, q, zero), jnp.where(lo, zero, q)], axis=0)
        s_diag = jnp.where(causal, _dot_nt(k_ref[past:here, :], qs), -jnp.inf)
        s_past = _dot_nt(k_ref[0:past, :], qs) if i > 0 else None
        return s_diag, s_past

    nq = seq // tq
    ahead = scores(0)
    for i in range(nq):
        past, here = i * tq, (i + 1) * tq
        s_diag, s_past = ahead
        if i + 1 < nq:
            ahead = scores(i + 1)
        m = jnp.max(s_diag, axis=0, keepdims=True)
        if i > 0:
            m = jnp.maximum(m, jnp.max(s_past, axis=0, keepdims=True))
        acc = _dot(vx_scr[:, past:here], jnp.exp2(s_diag - m).astype(BF16))
        if i > 0:
            acc = acc + _dot(vx_scr[:, 0:past], jnp.exp2(s_past - m).astype(BF16))
        ot = acc[0:PAIR] / acc[PAIR:PAIR + 1]
        ot = ot[:, 0:tq] - lam * ot[:, tq:2 * tq]
        out_ref[past:here, :] = _subln_gate(ot.T, subw_ref[...], dg_ref[past:here, :], lam_init)


def _attn_prompt(q, kb, vt, dg, subw, lam_vecs, *, batch, seq, tq, lam_init):
    m, width = q.shape
    blk = pl.BlockSpec((seq, PAIR), lambda b, h: (b, h))
    const = lambda b, h: (0, 0)
    return pl.pallas_call(
        functools.partial(_attn_prompt_kernel, tq=tq, lam_init=lam_init),
        grid=(batch, width // PAIR),
        in_specs=[blk, blk, pl.BlockSpec((None, PAIR, seq), lambda b, h: (b, h, 0)), blk,
                  pl.BlockSpec((1, PAIR), const), pl.BlockSpec((4, HEAD_DIM), const)],
        out_specs=blk,
        out_shape=jax.ShapeDtypeStruct((m, width), F32),
        scratch_shapes=[pltpu.VMEM((PAIR + BF16_ROWS, seq), BF16)],
        compiler_params=_params(("arbitrary", "arbitrary")),
        name="attn_prompt",
    )(q, kb, vt, dg, subw, lam_vecs)


def _attn_sample_steps(pt_ref, q_ref, kn_ref, vn_ref, dg_ref, subw_ref, lam_ref, ck_ref, cv_ref, out_ref,
                        kbuf, vbuf, sem, *, b, i, first, last, npages, tpad, heads, lam_init, nseq):
    prow = kbuf.shape[2]

    def fetch(seq, slot):
        out = []
        for j in range(npages):
            row = pl.multiple_of(pt_ref[seq * npages + j] * prow, prow)
            out.append(pltpu.make_async_copy(ck_ref.at[pl.ds(row, prow), :], kbuf.at[slot, j], sem.at[slot, 0]))
            out.append(pltpu.make_async_copy(cv_ref.at[pl.ds(row, prow), :], vbuf.at[slot, j], sem.at[slot, 1]))
        return out

    def wait(slot):
        for j in range(npages):
            pltpu.make_async_copy(ck_ref.at[pl.ds(0, prow), :], kbuf.at[slot, j], sem.at[slot, 0]).wait()
            pltpu.make_async_copy(cv_ref.at[pl.ds(0, prow), :], vbuf.at[slot, j], sem.at[slot, 1]).wait()

    if first:
        @pl.when(b == 0)
        def _():
            for s in range(PAGE_SLOTS):
                for c in fetch(min(s, nseq - 1), s):
                    c.start()

    slot = b % PAGE_SLOTS
    wait(slot)
    nrow = 2 * heads * tpad
    half = nrow // 2
    q = q_ref[i] * (1.0 / math.sqrt(HEAD_DIM))
    lo = lax.broadcasted_iota(jnp.int32, (tpad, PAIR), 1) < HEAD_DIM
    pieces = []
    for c in range(2):
        for h in range(heads):
            qh = q[:, h * PAIR:(h + 1) * PAIR]
            pieces.append(jnp.where(lo, qh, 0.0) if c == 0 else jnp.where(lo, 0.0, qh))
    qst = jnp.concatenate(pieces, axis=0).astype(BF16)

    def two(buf, j):
        return jnp.concatenate([buf[slot, j].astype(BF16), buf[slot, j + 1].astype(BF16)], axis=0)

    rhead = (lax.broadcasted_iota(jnp.int32, (nrow, 2 * prow), 0) // tpad) % heads
    chead = lax.broadcasted_iota(jnp.int32, (nrow, 2 * prow), 1) % heads
    own = rhead == chead
    scores = [jnp.where(own, _dot_nt(qst, two(kbuf, j)), -jnp.inf) for j in range(0, npages, 2)]

    def new_rows(ref):
        x = ref[i]
        xs = [x[:, h * PAIR:(h + 1) * PAIR] for h in range(heads)]
        xs.append(jnp.zeros((LANES - heads * tpad, PAIR), F32))
        return jnp.concatenate(xs, axis=0).astype(BF16)

    s_new = _dot_nt(qst, new_rows(kn_ref))
    rn = lax.broadcasted_iota(jnp.int32, (nrow, LANES), 0)
    cn = lax.broadcasted_iota(jnp.int32, (nrow, LANES), 1)
    visible = ((rn // tpad) % heads == cn // tpad) & (cn % tpad <= rn % tpad) & (cn < heads * tpad)
    s_new = jnp.where(visible, s_new, -jnp.inf)
    yield
    m = jnp.max(s_new, axis=-1, keepdims=True)
    for s in scores:
        m = jnp.maximum(m, jnp.max(s, axis=-1, keepdims=True))
    p_new = jnp.exp(s_new - m)
    l = jnp.sum(p_new, axis=-1, keepdims=True)
    probs = []
    for s in scores:
        p = jnp.exp(s - m)
        l = l + jnp.sum(p, axis=-1, keepdims=True)
        probs.append(p)
    inv_l = 1.0 / l
    w1 = inv_l[0:half]
    w2 = _lambda(lam_ref, lam_init) * inv_l[half:nrow]

    def mix(p):
        return (p[0:half] * w1 - p[half:nrow] * w2).astype(BF16)

    o = _dot(mix(p_new), new_rows(vn_ref))
    for idx, j in enumerate(range(0, npages, 2)):
        o = o + _dot(mix(probs[idx]), two(vbuf, j))
    dg = dg_ref[i]
    outs = [_subln_gate(o[h * tpad:(h + 1) * tpad], subw_ref[...], dg[:, h * PAIR:(h + 1) * PAIR], lam_init)
            for h in range(heads)]
    out_ref[i] = jnp.concatenate(outs, axis=-1)

    ahead = jnp.minimum(b + PAGE_SLOTS, nseq - 1)
    for c in fetch(ahead, slot):
        c.start()

    if last:
        @pl.when(b == nseq - 1)
        def _():
            for s in range(PAGE_SLOTS):
                wait(s)


def _proj_attn_kernel(pt_ref, *refs, n_proj_in, proj_kw, attn_kw):
    n0, n1 = n_proj_in, n_proj_in + N_ATTN_IN
    proj_out, attn_out, attn_scr = refs[n1:n1 + N_PROJ_OUT], refs[n1 + N_PROJ_OUT], refs[n1 + N_PROJ_OUT + 1:]
    nper = attn_out.shape[0]
    step = pl.program_id(0)
    attn = [_attn_sample_steps(pt_ref, *refs[n0:n1], attn_out, *attn_scr, b=step * nper + i, i=i, first=i == 0,
                               last=i == nper - 1, **attn_kw) for i in range(nper)]
    next(attn[0])
    _proj_kernel(*refs[0:n0], *proj_out, **proj_kw)
    for steps in attn:
        for _ in steps:
            pass


def _proj_prompt_attn_sample(x2d, norm_w, w_bf16, bounds, q, kn, vn, dg, subw, lam_vecs, cache_k, cache_v, page_table,
                             *, tm, seq, layer, depth, stacked, lam_init):
    m, d = x2d.shape
    c0, c1, c2, c3, c4, c5 = bounds
    nstep = m // tm
    nt = seq // tm
    pheads = (c3 - c2) // PAIR
    batch, tpad, dwidth = q.shape
    nper = batch // nstep
    assert nper * nstep == batch
    npages = page_table.shape[1]
    n_phys, page, heads = cache_k.shape[1], cache_k.shape[2], cache_k.shape[3]
    prow = page * heads
    ck = cache_k.reshape(-1, PAIR)
    cv = cache_v.reshape(-1, PAIR)
    pt = page_table.reshape(-1).astype(jnp.int32) + layer * n_phys

    row = lambda i, pt: (i, 0)
    const = lambda i, pt: (0, 0)
    any_spec = pl.BlockSpec(memory_space=pl.ANY)
    tok = pl.BlockSpec((nper, tpad, dwidth), lambda i, pt: (i, 0, 0))
    wide = lambda w, dt: (pl.BlockSpec((tm, w), row), jax.ShapeDtypeStruct((m, w), dt))
    proj_in = [pl.BlockSpec((tm, d), row), pl.BlockSpec((1, d), const),
               pl.BlockSpec(w_bf16.shape, const, pipeline_mode=pl.Buffered(1))]
    if layer == 0:
        st_spec = pl.BlockSpec((depth, tm * pheads, PAIR), lambda i, pt: (0, i, 0))
        aliases = {}
    else:
        st_spec = pl.BlockSpec((None, tm * pheads, PAIR), lambda i, pt: (layer, i, 0))
        proj_in += [any_spec, any_spec]
        aliases = {4: 6, 5: 7}
    outs = [wide(c0, F32), wide(c1 - c0, F32), wide(c2 - c1, BF16), wide(c3 - c2, BF16),
            (pl.BlockSpec((1, c4 - c3, tm), lambda i, pt: (i // nt, 0, i % nt)),
             jax.ShapeDtypeStruct((m // seq, c4 - c3, seq), BF16)),
            wide(c5 - c4, F32)]
    outs += [(st_spec, jax.ShapeDtypeStruct((depth, m * pheads, PAIR), F32))] * 2
    assert len(outs) == N_PROJ_OUT
    outs.append((tok, jax.ShapeDtypeStruct((batch, tpad, dwidth), F32)))
    grid_spec = pltpu.PrefetchScalarGridSpec(
        num_scalar_prefetch=1,
        grid=(nstep,),
        in_specs=proj_in + [tok, tok, tok, tok, pl.BlockSpec((1, PAIR), const), pl.BlockSpec((4, HEAD_DIM), const),
                            any_spec, any_spec],
        out_specs=[o[0] for o in outs],
        scratch_shapes=[pltpu.VMEM((PAGE_SLOTS, npages, prow, PAIR), F32),
                        pltpu.VMEM((PAGE_SLOTS, npages, prow, PAIR), F32),
                        pltpu.SemaphoreType.DMA((PAGE_SLOTS, 2))],
    )
    proj_kw = dict(bounds=bounds, prompt=True, layer=layer)
    attn_kw = dict(npages=npages, tpad=tpad, heads=heads, lam_init=lam_init, nseq=batch)
    return pl.pallas_call(
        functools.partial(_proj_attn_kernel, n_proj_in=len(proj_in), proj_kw=proj_kw, attn_kw=attn_kw),
        grid_spec=grid_spec,
        out_shape=[o[1] for o in outs],
        input_output_aliases=aliases,
        compiler_params=pltpu.CompilerParams(dimension_semantics=("arbitrary",), vmem_limit_bytes=VMEM_LIMIT_FUSED),
        name="proj_prompt_attn_sample",
    )(pt, x2d, norm_w.reshape(1, d), w_bf16, *stacked, q, kn, vn, dg, subw, lam_vecs, ck, cv)


def _outproj_kernel(x_ref, r_ref, d_ref, w_ref, fw_ref, o_ref, *, final):
    half = r_ref.shape[1]
    y = x_ref[...] + _dot(r_ref[...].astype(BF16), w_ref[0:half, :]) + _dot(d_ref[...].astype(BF16), w_ref[half:, :])
    if final:
        y = y * lax.rsqrt(jnp.mean(y * y, axis=-1, keepdims=True) + RMS_EPS) * fw_ref[...]
    o_ref[...] = y


def _outproj(x2d, rw, df, w_bf16, final_w, *, final, tm):
    m, d = x2d.shape
    half = rw.shape[1]
    row = lambda i: (i, 0)
    const = lambda i: (0, 0)
    return pl.pallas_call(
        functools.partial(_outproj_kernel, final=final),
        grid=(m // tm,),
        in_specs=[pl.BlockSpec((tm, d), row), pl.BlockSpec((tm, half), row), pl.BlockSpec((tm, half), row),
                  pl.BlockSpec(w_bf16.shape, const), pl.BlockSpec((1, d), const)],
        out_specs=pl.BlockSpec((tm, d), row),
        out_shape=jax.ShapeDtypeStruct((m, d), F32),
        compiler_params=_params(("arbitrary",)),
        name="outproj",
    )(x2d, rw, df, w_bf16, final_w.reshape(1, d))


def kernel(x_prompt, x_sample, cache_k, cache_v, page_table, state_rwkv, state_shift, norm_w, w_in, shift_mu, decay_w0, decay_lora_b, iclr_a0, iclr_lora_b, key_k, key_a, bonus_r_k, lnx_w, lnx_b, lam_q1, lam_k1, lam_q2, lam_k2, subln_w, w_out, final_norm_w):
    depth = w_in.shape[0]
    bp, seq, d = x_prompt.shape
    bs, tdec, _ = x_sample.shape
    width = decay_w0.shape[-1]
    ncols = shift_mu.shape[-1]
    rank_w, rank_a = decay_lora_b.shape[1], iclr_lora_b.shape[1]
    heads_r = width // HEAD_DIM
    heads_d = cache_k.shape[3]
    dwidth = heads_d * cache_v.shape[4]
    dqk = heads_d * cache_k.shape[4]
    assert rank_w == HEAD_DIM and rank_a == HEAD_DIM and ncols == 3 * width + PAIR
    assert cache_k.shape[4] == PAIR and cache_v.shape[4] == PAIR
    c0 = ncols
    c1 = c0 + width
    c2 = c1 + dqk
    c3 = c2 + dqk
    c4 = c3 + dwidth
    c5 = c4 + dwidth
    bounds = (c0, c1, c2, c3, c4, c5)
    assert w_in.shape[2] == c5

    tpad = SUBLANES
    chunk = 64
    tile = 256
    assert tdec <= tpad
    xp = x_prompt.reshape(bp * seq, d)
    xs = jnp.pad(x_sample, ((0, 0), (0, tpad - tdec), (0, 0))).reshape(bs * tpad, d)

    kst = vst = sst = None
    outs = {n: [] for n in ("sp", "hp", "ks", "vs", "hs")}
    for l in range(depth):
        lam_init = 0.8 - 0.6 * math.exp(-0.3 * l)
        w_l = w_in[l].astype(BF16)
        wo_l = w_out[l].astype(BF16)
        zeros = jnp.zeros((rank_w, width), F32)
        lora = jnp.concatenate([jnp.concatenate([decay_lora_b[l], zeros], axis=1),
                                jnp.concatenate([zeros, iclr_lora_b[l]], axis=1)], axis=0).astype(BF16)
        lp = dict(mu=shift_mu[l].reshape(1, ncols), w0=decay_w0[l].reshape(1, width), lora=lora,
                  a0=iclr_a0[l].reshape(1, width), key_k=key_k[l].reshape(1, width), key_a=key_a[l].reshape(1, width),
                  bonus=bonus_r_k[l].reshape(1, width), lnx_w=lnx_w[l].reshape(1, width),
                  lnx_b=lnx_b[l].reshape(1, width))
        lam_vecs = jnp.stack([lam_q1[l], lam_k1[l], lam_q2[l], lam_k2[l]], axis=0)
        subw = subln_w[l].reshape(1, PAIR)
        last = l == depth - 1

        cols_s, rg_s, q_s, k_s, v_s, dg_s = _proj_sample(xs, norm_w[l], w_l, bounds, tm=tile)
        r3 = lambda t: t.reshape(bs, tpad, -1)
        cols, rg, q, kb, vt, dg, kst, vst, df_s = _proj_prompt_attn_sample(
            xp, norm_w[l], w_l, bounds, r3(q_s), r3(k_s), r3(v_s), r3(dg_s), subw, lam_vecs, cache_k, cache_v,
            page_table, tm=tile, seq=seq, layer=l, depth=depth, stacked=() if l == 0 else (kst, vst),
            lam_init=lam_init)

        rw, s_new = _rwkv_prompt(cols, rg, jnp.zeros((bp, ncols), F32),
                                 jnp.zeros((bp, heads_r, HEAD_DIM, HEAD_DIM), F32), lp,
                                 nseq_total=bp, rows=chunk, nsub=4)
        df = _attn_prompt(q, kb, vt, dg, subw, lam_vecs, batch=bp, seq=seq, tq=tile, lam_init=lam_init)
        xp = _outproj(xp, rw, df, wo_l, final_norm_w, final=last, tm=2 * tile)
        outs["sp"].append(s_new)
        outs["hp"].append(cols.reshape(bp, seq, ncols)[:, -1, :])

        prev = jnp.pad(state_shift[l][:, None, :], ((0, 0), (0, tpad - 1), (0, 0))).reshape(bs * tpad, ncols)
        rw_s, sst = _rwkv_sample(cols_s, rg_s, prev, state_rwkv[l], lp, nseq_total=bs, rows=chunk, tlen=tpad, nsub=1,
                                 t_valid=tdec, layer=l, depth=depth, stacked=() if l == 0 else (sst,))
        xs = _outproj(xs, rw_s, df_s.reshape(bs * tpad, dwidth), wo_l, final_norm_w, final=last, tm=tile)
        outs["ks"].append(r3(k_s)[:, :tdec].reshape(bs, tdec, heads_d, PAIR))
        outs["vs"].append(r3(v_s)[:, :tdec].reshape(bs, tdec, heads_d, PAIR))
        outs["hs"].append(r3(cols_s)[:, tdec - 1, :])

    y_prompt = xp.reshape(bp, seq, d)
    y_sample = xs.reshape(bs, tpad, d)[:, :tdec]
    st = lambda n: jnp.stack(outs[n], axis=0)
    k_prompt = kst.reshape(depth, bp, seq, heads_d, PAIR)
    v_prompt = vst.reshape(depth, bp, seq, heads_d, PAIR)
    return (y_prompt, y_sample, k_prompt, v_prompt, st("sp"), st("hp"), st("ks"), st("vs"), sst, st("hs"))
```

```python
import functools
import math

import jax
import jax.numpy as jnp
from jax import lax
from jax.experimental import pallas as pl
from jax.experimental.pallas import tpu as pltpu

F32 = jnp.float32
BF16 = jnp.bfloat16

RMS_EPS = 1e-6
SUBLN_EPS = 1e-5
GN_EPS = 64e-5

LANES = 128
SUBLANES = 8
HEAD_DIM = 64
PAIR = 2 * HEAD_DIM
VMEM_LIMIT = 48 * 1024 * 1024
BF16_ROWS = 16
PAGE_SLOTS = 3
PROMPT_Q_SCALE = math.log2(math.e) / math.sqrt(HEAD_DIM)

_NT = (((1,), (1,)), ((), ()))
_TN = (((0,), (0,)), ((), ()))


def _dot(a, b):
    return jnp.dot(a, b, preferred_element_type=F32)


def _dot_nt(a, b):
    return lax.dot_general(a, b, _NT, preferred_element_type=F32)


def _dot_tn(a, b):
    return lax.dot_general(a, b, _TN, preferred_element_type=F32)


def _sigmoid(x):
    return 1.0 / (1.0 + jnp.exp(-x))


def _silu(x):
    return x * _sigmoid(x)


def _cat(xs, axis):
    return xs[0] if len(xs) == 1 else jnp.concatenate(xs, axis=axis)


def _params(sem):
    return pltpu.CompilerParams(dimension_semantics=sem, vmem_limit_bytes=VMEM_LIMIT)


def _store_token_head(dst, val, heads):
    rows = val.shape[0]
    for h in range(heads):
        dst[pl.ds(h, rows, stride=heads), :] = val[:, h * PAIR:(h + 1) * PAIR]


def _proj_kernel(x_ref, nw_ref, w_ref, *refs, bounds, prompt, layer):
    x = x_ref[...]
    h = x * lax.rsqrt(jnp.mean(x * x, axis=-1, keepdims=True) + RMS_EPS) * nw_ref[...]
    hb = h.astype(BF16)
    c0, c1, c2, c3, c4, c5 = bounds
    mm = lambda lo, hi: _dot(hb, w_ref[:, lo:hi])
    if prompt:
        cols_ref, rg_ref, q_ref, kb_ref, vt_ref, dg_ref, kst_ref, vst_ref = refs[-8:]
    else:
        cols_ref, rg_ref, q_ref, k_ref, v_ref, dg_ref = refs
    cols_ref[...] = mm(0, c0)
    rg_ref[...] = mm(c0, c1)
    q = mm(c1, c2)
    k = mm(c2, c3)
    v = mm(c3, c4)
    dg_ref[...] = mm(c4, c5)
    if not prompt:
        q_ref[...] = q
        k_ref[...] = k
        v_ref[...] = v
        return
    q_ref[...] = (q * PROMPT_Q_SCALE).astype(BF16)
    kb_ref[...] = k.astype(BF16)
    vt_ref[0] = v.T.astype(BF16)
    heads = (c3 - c2) // PAIR
    if layer == 0:
        for slot in range(kst_ref.shape[0]):
            _store_token_head(kst_ref.at[slot], k, heads)
            _store_token_head(vst_ref.at[slot], v, heads)
    else:
        _store_token_head(kst_ref, k, heads)
        _store_token_head(vst_ref, v, heads)


def _proj(x2d, norm_w, w_bf16, bounds, *, tm, prompt, seq=None, layer=0, depth=1, stacked=()):
    m, d = x2d.shape
    c0, c1, c2, c3, c4, c5 = bounds
    row = lambda i: (i, 0)
    const = lambda i: (0, 0)
    wide = lambda w, dt: (pl.BlockSpec((tm, w), row), jax.ShapeDtypeStruct((m, w), dt))
    in_specs = [pl.BlockSpec((tm, d), row), pl.BlockSpec((1, d), const),
                pl.BlockSpec(w_bf16.shape, const, pipeline_mode=pl.Buffered(1))]
    aliases = {}
    if prompt:
        nt = seq // tm
        heads = (c3 - c2) // PAIR
        outs = [wide(c0, F32), wide(c1 - c0, F32), wide(c2 - c1, BF16), wide(c3 - c2, BF16),
                (pl.BlockSpec((1, c4 - c3, tm), lambda i: (i // nt, 0, i % nt)),
                 jax.ShapeDtypeStruct((m // seq, c4 - c3, seq), BF16)),
                wide(c5 - c4, F32)]
        if layer == 0:
            st_spec = pl.BlockSpec((depth, tm * heads, PAIR), lambda i: (0, i, 0))
        else:
            st_spec = pl.BlockSpec((None, tm * heads, PAIR), lambda i: (layer, i, 0))
            in_specs += [pl.BlockSpec(memory_space=pl.ANY)] * 2
            aliases = {3: 6, 4: 7}
        outs += [(st_spec, jax.ShapeDtypeStruct((depth, m * heads, PAIR), F32))] * 2
    else:
        outs = [wide(c0, F32), wide(c1 - c0, F32), wide(c2 - c1, F32), wide(c3 - c2, F32), wide(c4 - c3, F32),
                wide(c5 - c4, F32)]
    return pl.pallas_call(
        functools.partial(_proj_kernel, bounds=bounds, prompt=prompt, layer=layer),
        grid=(m // tm,),
        in_specs=in_specs,
        out_specs=[o[0] for o in outs],
        out_shape=[o[1] for o in outs],
        input_output_aliases=aliases,
        compiler_params=_params(("arbitrary",)),
        name="proj_prompt" if prompt else "proj_sample",
    )(x2d, norm_w.reshape(1, d), w_bf16, *stacked)


def _head_sum(x):
    rows, width = x.shape
    lo = lax.broadcasted_iota(jnp.int32, (rows, PAIR), 1) < HEAD_DIM
    out = []
    for p in range(width // PAIR):
        xp = x[:, p * PAIR:(p + 1) * PAIR]
        s_lo = jnp.sum(jnp.where(lo, xp, 0.0), axis=-1, keepdims=True)
        s_hi = jnp.sum(jnp.where(lo, 0.0, xp), axis=-1, keepdims=True)
        out.append(jnp.where(lo, s_lo, s_hi))
    return jnp.concatenate(out, axis=-1)


def _split3(x):
    h0 = x.astype(BF16)
    r1 = x - h0.astype(F32)
    h1 = r1.astype(BF16)
    h2 = (r1 - h1.astype(F32)).astype(BF16)
    return h0, h1, h2


N_RWKV_IN = 13
N_ATTN_IN = 8


def _blockdiag(s_even, s_odd):
    zero = jnp.zeros((HEAD_DIM, HEAD_DIM), F32)
    return jnp.concatenate([jnp.concatenate([s_even, zero], axis=1), jnp.concatenate([zero, s_odd], axis=1)], axis=0)


def _rwkv_kernel(cols_ref, gate_ref, prev_ref, s0_ref, mu_ref, w0_ref, lora_ref, a0_ref, keyk_ref, keya_ref,
                 bonus_ref, lnw_ref, lnb_ref, *refs, rows, tlen, nsub, width, t_valid, carry, nslot):
    out_ref, sout_ref, s_scr, prev_scr = refs[-4:]
    tb = pl.program_id(1)
    nblk = pl.num_programs(1)
    npair = width // PAIR
    nseq = rows // tlen
    r2 = 2 * rows
    blk_rows = rows * nsub

    c = cols_ref[...]
    row_t = lax.broadcasted_iota(jnp.int32, c.shape, 0) % tlen if not carry else None
    if carry:
        @pl.when(tb == 0)
        def _():
            prev_scr[...] = prev_ref[0]
            for p in range(npair):
                s_scr[p] = _blockdiag(s0_ref[0, 2 * p], s0_ref[0, 2 * p + 1])

        row = lax.broadcasted_iota(jnp.int32, c.shape, 0)
        shifted = jnp.where(row == 0, prev_scr[...], pltpu.roll(c, 1, 0))
        prev_scr[...] = c[blk_rows - 1:blk_rows, :]
        state = {(0, p): s_scr[p] for p in range(npair)}
    else:
        shifted = jnp.where(row_t == 0, prev_ref[...], pltpu.roll(c, 1, 0))
        state = {(q, p): _blockdiag(s0_ref[q, 2 * p], s0_ref[q, 2 * p + 1])
                 for q in range(nsub * nseq) for p in range(npair)}

    xs = c + (shifted - c) * mu_ref[...]
    r = xs[:, 0:width]
    k = xs[:, width:2 * width]
    v = xs[:, 2 * width:3 * width]
    xwa = xs[:, 3 * width:3 * width + PAIR]
    lo_rank = lax.broadcasted_iota(jnp.int32, xwa.shape, 1) < HEAD_DIM
    xwa = jnp.where(lo_rank, jnp.tanh(xwa), xwa)
    lora = _dot(xwa.astype(BF16), lora_ref[...])
    z = -(w0_ref[...] + lora[:, 0:width])
    softplus = jnp.maximum(z, 0.0) + jnp.log(1.0 + jnp.exp(-jnp.abs(z)))
    lw = -jnp.exp(-softplus - 0.5)
    a = _sigmoid(a0_ref[...] + lora[:, width:2 * width])
    kk = k * keyk_ref[...]
    kk = kk / jnp.maximum(jnp.sqrt(_head_sum(kk * kk)), 1e-12)
    k2 = k * (1.0 + (a - 1.0) * keya_ref[...])
    av = -kk
    bv = kk * a
    if t_valid < tlen:
        live = lax.broadcasted_iota(jnp.int32, lw.shape, 0) % tlen < t_valid
        lw = jnp.where(live, lw, 0.0)
        av = jnp.where(live, av, 0.0)
        bv = jnp.where(live, bv, 0.0)
        k2s = jnp.where(live, k2, 0.0)
        vs = jnp.where(live, v, 0.0)
    else:
        k2s, vs = k2, v

    ri = lax.broadcasted_iota(jnp.int32, (r2, r2), 0)
    ci = lax.broadcasted_iota(jnp.int32, (r2, r2), 1)
    same = (ri // tlen) == (ci // tlen)
    strict = same & (ci < ri)
    incl = same & (ci <= ri)
    eye = (ri == ci).astype(F32)
    cum_lower = incl[0:rows, 0:rows]
    cum = jnp.concatenate([cum_lower, same[0:rows, 0:rows]], axis=0).astype(F32).astype(BF16)
    lo = lax.broadcasted_iota(jnp.int32, (rows, PAIR), 1) < HEAD_DIM
    nsteps = max(1, (tlen - 1).bit_length())

    def stack(xp):
        return jnp.concatenate([jnp.where(lo, xp, 0.0), jnp.where(lo, 0.0, xp)], axis=0)

    def seq_rows(x, q):
        if nseq == 1:
            return x
        return jnp.concatenate([x[q * tlen:(q + 1) * tlen], x[rows + q * tlen:rows + (q + 1) * tlen]], axis=0)

    def unseq(parts):
        if nseq == 1:
            return parts[0]
        return jnp.concatenate([x[0:tlen] for x in parts] + [x[tlen:2 * tlen] for x in parts], axis=0)

    probs = []
    for sub in range(nsub):
        sl = slice(sub * rows, (sub + 1) * rows)
        lwc = lw[sl]
        g0, g1, g2 = _split3(lwc)
        gg = _dot(cum, g0) + _dot(cum, g1) + _dot(cum, g2)
        g = gg[0:rows]
        g_end = gg[rows:r2]
        eg = jnp.exp(g)
        eng = jnp.exp(-g)
        egc = jnp.exp(g_end - g)
        at = av[sl] * jnp.exp(g - lwc)
        rt = r[sl] * eg
        bt = bv[sl] * eng
        kt = k2s[sl] * eng
        bh = bv[sl] * egc
        kh = k2s[sl] * egc
        e_end = jnp.exp(g_end)
        vc = vs[sl]
        for p in range(npair):
            ps = slice(p * PAIR, (p + 1) * PAIR)
            probs.append(dict(
                sub=sub, p=p,
                la=stack(at[:, ps]).astype(BF16), lr=stack(rt[:, ps]), lb=stack(bt[:, ps]).astype(BF16),
                lk=stack(kt[:, ps]).astype(BF16), lbh=stack(bh[:, ps]), lkh=stack(kh[:, ps]), lv=stack(vc[:, ps]),
                e_end=e_end[:, ps]))
    for pr in probs:
        pr["lrb"] = pr["lr"].astype(BF16)
        pr["lvb"] = pr["lv"].astype(BF16)
        sc = _dot_nt(jnp.concatenate([pr["la"], pr["lrb"]], axis=0),
                     jnp.concatenate([pr["lb"], pr["lk"]], axis=0))
        pr["pw"] = jnp.where(strict, sc[0:r2, 0:r2], 0.0)
        pr["dak"] = jnp.where(strict, sc[0:r2, r2:2 * r2], 0.0)
        pr["drb"] = jnp.where(incl, sc[r2:2 * r2, 0:r2], 0.0).astype(BF16)
        pr["drk"] = jnp.where(incl, sc[r2:2 * r2, r2:2 * r2], 0.0)
        pr["inv"] = eye + pr["pw"]
    for step in range(nsteps):
        for pr in probs:
            pb = pr["pw"].astype(BF16)
            if step == 0:
                if nsteps > 1:
                    pr["pw"] = _dot(pb, pb)
            elif step < nsteps - 1:
                px = _dot(pb, jnp.concatenate([pb, pr["inv"].astype(BF16)], axis=1))
                pr["pw"] = px[:, 0:r2]
                pr["inv"] = pr["inv"] + px[:, r2:2 * r2]
            else:
                pr["inv"] = pr["inv"] + _dot(pb, pr["inv"].astype(BF16))
    for pr in probs:
        pr["dv"] = _dot(jnp.concatenate([pr["dak"], pr["drk"]], axis=0).astype(BF16), pr["lvb"])
    for pr in probs:
        tw = _dot(pr["inv"].astype(BF16), jnp.concatenate([pr["la"], pr["dv"][0:r2].astype(BF16)], axis=1))
        pr["lw"] = tw[:, 0:PAIR]
        pr["lu0"] = tw[:, PAIR:2 * PAIR]

    y_chunks = []
    for sub in range(nsub):
        sub_probs = [pr for pr in probs if pr["sub"] == sub]
        for pr in sub_probs:
            pr["ws"] = []
            for q in range(nseq):
                key = (0 if carry else sub * nseq + q, pr["p"])
                lhs = jnp.concatenate([seq_rows(pr["lw"], q), seq_rows(pr["lr"], q)], axis=0).astype(BF16)
                pr["ws"].append(_dot_nt(lhs, state[key].astype(BF16)))
        for pr in sub_probs:
            lu = unseq([w[0:2 * tlen] for w in pr["ws"]]) + pr["lu0"]
            pr["lu"] = lu
            y2 = unseq([w[2 * tlen:4 * tlen] for w in pr["ws"]]) + _dot(pr["drb"], lu.astype(BF16)) + pr["dv"][r2:2 * r2]
            pr["y"] = y2[0:rows] + y2[rows:r2]
        for pr in sub_probs:
            for q in range(nseq):
                key = (0 if carry else sub * nseq + q, pr["p"])
                lhs = jnp.concatenate([seq_rows(pr["lu"], q), seq_rows(pr["lv"], q)], axis=0).astype(BF16)
                rhs = jnp.concatenate([seq_rows(pr["lbh"], q), seq_rows(pr["lkh"], q)], axis=0).astype(BF16)
                decay = pr["e_end"][q * tlen:q * tlen + 1, :]
                state[key] = state[key] * decay + _dot_tn(lhs, rhs)
        y_chunks.append(_cat([pr["y"] for pr in sub_probs], axis=-1))
    y = _cat(y_chunks, axis=0)

    inv_n = 1.0 / HEAD_DIM
    mean = _head_sum(y) * inv_n
    d = y - mean
    var = _head_sum(d * d) * inv_n
    yn = d * lax.rsqrt(var + GN_EPS) * lnw_ref[...] + lnb_ref[...]
    bonus = _head_sum(r * k2 * bonus_ref[...]) * v
    out_ref[...] = (yn + bonus) * _silu(gate_ref[...])

    views = [sout_ref.at[s] for s in range(sout_ref.shape[0])] if nslot else [sout_ref]

    def write_state(q, src):
        for p in range(npair):
            sp = src[(q, p)]
            for dst in views:
                dst[q, 2 * p] = sp[0:HEAD_DIM, 0:HEAD_DIM]
                dst[q, 2 * p + 1] = sp[HEAD_DIM:PAIR, HEAD_DIM:PAIR]

    if carry:
        for p in range(npair):
            s_scr[p] = state[(0, p)]

        @pl.when(tb == nblk - 1)
        def _():
            write_state(0, state)
    else:
        for q in range(nsub * nseq):
            write_state(q, state)


def _rwkv_sample(cols, gate, prev, s0, lp, *, nseq_total, rows, tlen, nsub, t_valid, layer, depth, stacked=()):
    m, ncols = cols.shape
    width = gate.shape[1]
    blk_rows = rows * nsub
    const = lambda b, t: (0, 0)
    vec = lambda n: pl.BlockSpec((1, n), const)
    hd = (width // HEAD_DIM, HEAD_DIM, HEAD_DIM)
    blk = lambda b, t: (b, 0)
    sblk = blk_rows // tlen
    if layer == 0:
        nslot, extra_specs, aliases = depth, [], {}
        sout_spec = pl.BlockSpec((depth, sblk) + hd, lambda b, t: (0, b, 0, 0, 0))
    else:
        nslot, extra_specs, aliases = 0, [pl.BlockSpec(memory_space=pl.ANY)], {N_RWKV_IN: 1}
        sout_spec = pl.BlockSpec((None, sblk) + hd, lambda b, t: (layer, b, 0, 0, 0))
    return pl.pallas_call(
        functools.partial(_rwkv_kernel, rows=rows, tlen=tlen, nsub=nsub, width=width, t_valid=t_valid, carry=False,
                          nslot=nslot),
        grid=(m // blk_rows, 1),
        in_specs=[pl.BlockSpec((blk_rows, ncols), blk), pl.BlockSpec((blk_rows, width), blk),
                  pl.BlockSpec((blk_rows, ncols), blk), pl.BlockSpec((sblk,) + hd, lambda b, t: (b, 0, 0, 0)),
                  vec(ncols), vec(width), pl.BlockSpec((PAIR, 2 * width), const), vec(width), vec(width),
                  vec(width), vec(width), vec(width), vec(width)] + extra_specs,
        out_specs=[pl.BlockSpec((blk_rows, width), blk), sout_spec],
        out_shape=[jax.ShapeDtypeStruct((m, width), F32), jax.ShapeDtypeStruct((depth, nseq_total) + hd, F32)],
        input_output_aliases=aliases,
        scratch_shapes=[pltpu.VMEM((width // PAIR, PAIR, PAIR), F32), pltpu.VMEM((1, ncols), F32)],
        compiler_params=_params(("arbitrary", "arbitrary")),
        name="rwkv_sample",
    )(cols, gate, prev, s0, lp["mu"], lp["w0"], lp["lora"], lp["a0"], lp["key_k"],
      lp["key_a"], lp["bonus"], lp["lnx_w"], lp["lnx_b"], *stacked)


def _lambda(lam_ref, lam_init):
    lam = lam_ref[...]
    e1 = jnp.exp(jnp.sum(lam[0:1] * lam[1:2], axis=-1, keepdims=True))
    e2 = jnp.exp(jnp.sum(lam[2:3] * lam[3:4], axis=-1, keepdims=True))
    return e1 - e2 + lam_init


def _subln_gate(o, subw, gate, lam_init):
    o = o * lax.rsqrt(jnp.mean(o * o, axis=-1, keepdims=True) + SUBLN_EPS) * subw
    return o * (1.0 - lam_init) * _silu(gate)


def _attn_prompt_kernel(q_ref, k_ref, vt_ref, dg_ref, subw_ref, lam_ref, out_ref, vx_scr, *, tq, lam_init):
    seq = q_ref.shape[0]
    lo = lax.broadcasted_iota(jnp.int32, (tq, PAIR), 1) < HEAD_DIM
    ki = lax.broadcasted_iota(jnp.int32, (tq, 2 * tq), 0)
    qi = lax.broadcasted_iota(jnp.int32, (tq, 2 * tq), 1)
    causal = ki <= jnp.where(qi >= tq, qi - tq, qi)
    lam = _lambda(lam_ref, lam_init)
    vx_scr[0:PAIR, :] = vt_ref[...]
    vx_scr[PAIR:PAIR + BF16_ROWS, :] = jnp.ones((BF16_ROWS, seq), BF16)
    def scores(i):
        past, here = i * tq, (i + 1) * tq
        q = q_ref[past:here, :]
        zero = jnp.zeros_like(q)
        qs = jnp.concatenate([jnp.where(lo, q, zero), jnp.where(lo, zero, q)], axis=0)
        s_diag = jnp.where(causal, _dot_nt(k_ref[past:here, :], qs), -jnp.inf)
        s_past = _dot_nt(k_ref[0:past, :], qs) if i > 0 else None
        return s_diag, s_past

    nq = seq // tq
    ahead = scores(0)
    for i in range(nq):
        past, here = i * tq, (i + 1) * tq
        s_diag, s_past = ahead
        if i + 1 < nq:
            ahead = scores(i + 1)
        m = jnp.max(s_diag, axis=0, keepdims=True)
        if i > 0:
            m = jnp.maximum(m, jnp.max(s_past, axis=0, keepdims=True))
        acc = _dot(vx_scr[:, past:here], jnp.exp2(s_diag - m).astype(BF16))
        if i > 0:
            acc = acc + _dot(vx_scr[:, 0:past], jnp.exp2(s_past - m).astype(BF16))
        ot = acc[0:PAIR] / acc[PAIR:PAIR + 1]
        ot = ot[:, 0:tq] - lam * ot[:, tq:2 * tq]
        out_ref[past:here, :] = _subln_gate(ot.T, subw_ref[...], dg_ref[past:here, :], lam_init)


def _attn_prompt(q, kb, vt, dg, subw, lam_vecs, *, batch, seq, tq, lam_init):
    m, width = q.shape
    blk = pl.BlockSpec((seq, PAIR), lambda b, h: (b, h))
    const = lambda b, h: (0, 0)
    return pl.pallas_call(
        functools.partial(_attn_prompt_kernel, tq=tq, lam_init=lam_init),
        grid=(batch, width // PAIR),
        in_specs=[blk, blk, pl.BlockSpec((None, PAIR, seq), lambda b, h: (b, h, 0)), blk,
                  pl.BlockSpec((1, PAIR), const), pl.BlockSpec((4, HEAD_DIM), const)],
        out_specs=blk,
        out_shape=jax.ShapeDtypeStruct((m, width), F32),
        scratch_shapes=[pltpu.VMEM((PAIR + BF16_ROWS, seq), BF16)],
        compiler_params=_params(("arbitrary", "arbitrary")),
        name="attn_prompt",
    )(q, kb, vt, dg, subw, lam_vecs)


def _attn_sample_steps(pt_ref, q_ref, kn_ref, vn_ref, dg_ref, subw_ref, lam_ref, ck_ref, cv_ref, out_ref,
                        kbuf, vbuf, sem, *, b, i, first, last, npages, tpad, heads, lam_init, nseq):
    prow = kbuf.shape[2]

    def fetch(seq, slot):
        out = []
        for j in range(npages):
            row = pl.multiple_of(pt_ref[seq * npages + j] * prow, prow)
            out.append(pltpu.make_async_copy(ck_ref.at[pl.ds(row, prow), :], kbuf.at[slot, j], sem.at[slot, 0]))
            out.append(pltpu.make_async_copy(cv_ref.at[pl.ds(row, prow), :], vbuf.at[slot, j], sem.at[slot, 1]))
        return out

    def wait(slot):
        for j in range(npages):
            pltpu.make_async_copy(ck_ref.at[pl.ds(0, prow), :], kbuf.at[slot, j], sem.at[slot, 0]).wait()
            pltpu.make_async_copy(cv_ref.at[pl.ds(0, prow), :], vbuf.at[slot, j], sem.at[slot, 1]).wait()

    if first:
        @pl.when(b == 0)
        def _():
            for s in range(PAGE_SLOTS):
                for c in fetch(min(s, nseq - 1), s):
                    c.start()

    slot = b % PAGE_SLOTS
    wait(slot)
    nrow = 2 * heads * tpad
    half = nrow // 2
    q = q_ref[i] * (1.0 / math.sqrt(HEAD_DIM))
    lo = lax.broadcasted_iota(jnp.int32, (tpad, PAIR), 1) < HEAD_DIM
    pieces = []
    for c in range(2):
        for h in range(heads):
            qh = q[:, h * PAIR:(h + 1) * PAIR]
            pieces.append(jnp.where(lo, qh, 0.0) if c == 0 else jnp.where(lo, 0.0, qh))
    qst = jnp.concatenate(pieces, axis=0).astype(BF16)

    def two(buf, j):
        return jnp.concatenate([buf[slot, j].astype(BF16), buf[slot, j + 1].astype(BF16)], axis=0)

    rhead = (lax.broadcasted_iota(jnp.int32, (nrow, 2 * prow), 0) // tpad) % heads
    chead = lax.broadcasted_iota(jnp.int32, (nrow, 2 * prow), 1) % heads
    own = rhead == chead
    scores = [jnp.where(own, _dot_nt(qst, two(kbuf, j)), -jnp.inf) for j in range(0, npages, 2)]

    def new_rows(ref):
        x = ref[i]
        xs = [x[:, h * PAIR:(h + 1) * PAIR] for h in range(heads)]
        xs.append(jnp.zeros((LANES - heads * tpad, PAIR), F32))
        return jnp.concatenate(xs, axis=0).astype(BF16)

    s_new = _dot_nt(qst, new_rows(kn_ref))
    rn = lax.broadcasted_iota(jnp.int32, (nrow, LANES), 0)
    cn = lax.broadcasted_iota(jnp.int32, (nrow, LANES), 1)
    visible = ((rn // tpad) % heads == cn // tpad) & (cn % tpad <= rn % tpad) & (cn < heads * tpad)
    s_new = jnp.where(visible, s_new, -jnp.inf)
    yield
    m = jnp.max(s_new, axis=-1, keepdims=True)
    for s in scores:
        m = jnp.maximum(m, jnp.max(s, axis=-1, keepdims=True))
    p_new = jnp.exp(s_new - m)
    l = jnp.sum(p_new, axis=-1, keepdims=True)
    probs = []
    for s in scores:
        p = jnp.exp(s - m)
        l = l + jnp.sum(p, axis=-1, keepdims=True)
        probs.append(p)
    inv_l = 1.0 / l
    w1 = inv_l[0:half]
    w2 = _lambda(lam_ref, lam_init) * inv_l[half:nrow]

    def mix(p):
        return (p[0:half] * w1 - p[half:nrow] * w2).astype(BF16)

    o = _dot(mix(p_new), new_rows(vn_ref))
    for idx, j in enumerate(range(0, npages, 2)):
        o = o + _dot(mix(probs[idx]), two(vbuf, j))
    dg = dg_ref[i]
    outs = [_subln_gate(o[h * tpad:(h + 1) * tpad], subw_ref[...], dg[:, h * PAIR:(h + 1) * PAIR], lam_init)
            for h in range(heads)]
    out_ref[i] = jnp.concatenate(outs, axis=-1)

    ahead = jnp.minimum(b + PAGE_SLOTS, nseq - 1)
    for c in fetch(ahead, slot):
        c.start()

    if last:
        @pl.when(b == nseq - 1)
        def _():
            for s in range(PAGE_SLOTS):
                wait(s)


def _rwkv_attn_kernel(pt_ref, *refs, rwkv_kw, attn_kw):
    n0, n1 = N_RWKV_IN, N_RWKV_IN + N_ATTN_IN
    rwkv_out, attn_out = refs[n1:n1 + 2], refs[n1 + 2]
    rwkv_scr, attn_scr = refs[n1 + 3:n1 + 5], refs[n1 + 5:n1 + 8]
    nper = attn_out.shape[0]
    step = pl.program_id(0) * pl.num_programs(1) + pl.program_id(1)
    attn = [_attn_sample_steps(pt_ref, *refs[n0:n1], attn_out, *attn_scr, b=step * nper + i, i=i, first=i == 0,
                               last=i == nper - 1, **attn_kw) for i in range(nper)]
    next(attn[0])
    _rwkv_kernel(*refs[0:n0], *rwkv_out, *rwkv_scr, **rwkv_kw)
    for steps in attn:
        for _ in steps:
            pass


def _rwkv_prompt_attn_sample(cols, gate, prev, s0, lp, q, kn, vn, dg, subw, lam_vecs, cache_k, cache_v, page_table,
                             *, nseq_total, rows, nsub, layer, lam_init):
    m, ncols = cols.shape
    width = gate.shape[1]
    blk_rows = rows * nsub
    nblk = m // nseq_total // blk_rows
    hd = (width // HEAD_DIM, HEAD_DIM, HEAD_DIM)
    batch, tpad, dwidth = q.shape
    nper = batch // (nseq_total * nblk)
    assert nper * nseq_total * nblk == batch
    npages = page_table.shape[1]
    n_phys, page, heads = cache_k.shape[1], cache_k.shape[2], cache_k.shape[3]
    prow = page * heads
    ck = cache_k.reshape(-1, PAIR)
    cv = cache_v.reshape(-1, PAIR)
    pt = page_table.reshape(-1).astype(jnp.int32) + layer * n_phys

    const = lambda b, t, pt: (0, 0)
    vec = lambda n: pl.BlockSpec((1, n), const)
    blk = lambda b, t, pt: (b * nblk + t, 0)
    state_spec = pl.BlockSpec((1,) + hd, lambda b, t, pt: (b, 0, 0, 0))
    tok = pl.BlockSpec((nper, tpad, dwidth), lambda b, t, pt: (b * nblk + t, 0, 0))
    any_spec = pl.BlockSpec(memory_space=pl.ANY)
    grid_spec = pltpu.PrefetchScalarGridSpec(
        num_scalar_prefetch=1,
        grid=(nseq_total, nblk),
        in_specs=[pl.BlockSpec((blk_rows, ncols), blk), pl.BlockSpec((blk_rows, width), blk),
                  pl.BlockSpec((1, 1, ncols), lambda b, t, pt: (b, 0, 0)), state_spec,
                  vec(ncols), vec(width), pl.BlockSpec((PAIR, 2 * width), const), vec(width), vec(width),
                  vec(width), vec(width), vec(width), vec(width),
                  tok, tok, tok, tok, pl.BlockSpec((1, PAIR), const), pl.BlockSpec((4, HEAD_DIM), const),
                  any_spec, any_spec],
        out_specs=[pl.BlockSpec((blk_rows, width), blk), state_spec, tok],
        scratch_shapes=[pltpu.VMEM((width // PAIR, PAIR, PAIR), F32), pltpu.VMEM((1, ncols), F32),
                        pltpu.VMEM((PAGE_SLOTS, npages, prow, PAIR), F32),
                        pltpu.VMEM((PAGE_SLOTS, npages, prow, PAIR), F32),
                        pltpu.SemaphoreType.DMA((PAGE_SLOTS, 2))],
    )
    rwkv_kw = dict(rows=rows, tlen=rows, nsub=nsub, width=width, t_valid=rows, carry=True, nslot=0)
    attn_kw = dict(npages=npages, tpad=tpad, heads=heads, lam_init=lam_init, nseq=batch)
    return pl.pallas_call(
        functools.partial(_rwkv_attn_kernel, rwkv_kw=rwkv_kw, attn_kw=attn_kw),
        grid_spec=grid_spec,
        out_shape=[jax.ShapeDtypeStruct((m, width), F32), jax.ShapeDtypeStruct((nseq_total,) + hd, F32),
                   jax.ShapeDtypeStruct((batch, tpad, dwidth), F32)],
        compiler_params=_params(("arbitrary", "arbitrary")),
        name="rwkv_prompt_attn_sample",
    )(pt, cols, gate, prev.reshape(nseq_total, 1, ncols), s0, lp["mu"], lp["w0"], lp["lora"], lp["a0"], lp["key_k"],
      lp["key_a"], lp["bonus"], lp["lnx_w"], lp["lnx_b"], q, kn, vn, dg, subw, lam_vecs, ck, cv)


def _outproj_kernel(x_ref, r_ref, d_ref, w_ref, fw_ref, o_ref, *, final):
    half = r_ref.shape[1]
    y = x_ref[...] + _dot(r_ref[...].astype(BF16), w_ref[0:half, :]) + _dot(d_ref[...].astype(BF16), w_ref[half:, :])
    if final:
        y = y * lax.rsqrt(jnp.mean(y * y, axis=-1, keepdims=True) + RMS_EPS) * fw_ref[...]
    o_ref[...] = y


def _outproj(x2d, rw, df, w_bf16, final_w, *, final, tm):
    m, d = x2d.shape
    half = rw.shape[1]
    row = lambda i: (i, 0)
    const = lambda i: (0, 0)
    return pl.pallas_call(
        functools.partial(_outproj_kernel, final=final),
        grid=(m // tm,),
        in_specs=[pl.BlockSpec((tm, d), row), pl.BlockSpec((tm, half), row), pl.BlockSpec((tm, half), row),
                  pl.BlockSpec(w_bf16.shape, const), pl.BlockSpec((1, d), const)],
        out_specs=pl.BlockSpec((tm, d), row),
        out_shape=jax.ShapeDtypeStruct((m, d), F32),
        compiler_params=_params(("arbitrary",)),
        name="outproj",
    )(x2d, rw, df, w_bf16, final_w.reshape(1, d))


def kernel(x_prompt, x_sample, cache_k, cache_v, page_table, state_rwkv, state_shift, norm_w, w_in, shift_mu, decay_w0, decay_lora_b, iclr_a0, iclr_lora_b, key_k, key_a, bonus_r_k, lnx_w, lnx_b, lam_q1, lam_k1, lam_q2, lam_k2, subln_w, w_out, final_norm_w):
    depth = w_in.shape[0]
    bp, seq, d = x_prompt.shape
    bs, tdec, _ = x_sample.shape
    width = decay_w0.shape[-1]
    ncols = shift_mu.shape[-1]
    rank_w, rank_a = decay_lora_b.shape[1], iclr_lora_b.shape[1]
    heads_r = width // HEAD_DIM
    heads_d = cache_k.shape[3]
    dwidth = heads_d * cache_v.shape[4]
    dqk = heads_d * cache_k.shape[4]
    assert rank_w == HEAD_DIM and rank_a == HEAD_DIM and ncols == 3 * width + PAIR
    assert cache_k.shape[4] == PAIR and cache_v.shape[4] == PAIR
    c0 = ncols
    c1 = c0 + width
    c2 = c1 + dqk
    c3 = c2 + dqk
    c4 = c3 + dwidth
    c5 = c4 + dwidth
    bounds = (c0, c1, c2, c3, c4, c5)
    assert w_in.shape[2] == c5

    tpad = SUBLANES
    chunk = 64
    tile = 256
    assert tdec <= tpad
    xp = x_prompt.reshape(bp * seq, d)
    xs = jnp.pad(x_sample, ((0, 0), (0, tpad - tdec), (0, 0))).reshape(bs * tpad, d)

    kst = vst = sst = None
    outs = {n: [] for n in ("sp", "hp", "ks", "vs", "hs")}
    for l in range(depth):
        lam_init = 0.8 - 0.6 * math.exp(-0.3 * l)
        w_l = w_in[l].astype(BF16)
        wo_l = w_out[l].astype(BF16)
        zeros = jnp.zeros((rank_w, width), F32)
        lora = jnp.concatenate([jnp.concatenate([decay_lora_b[l], zeros], axis=1),
                                jnp.concatenate([zeros, iclr_lora_b[l]], axis=1)], axis=0).astype(BF16)
        lp = dict(mu=shift_mu[l].reshape(1, ncols), w0=decay_w0[l].reshape(1, width), lora=lora,
                  a0=iclr_a0[l].reshape(1, width), key_k=key_k[l].reshape(1, width), key_a=key_a[l].reshape(1, width),
                  bonus=bonus_r_k[l].reshape(1, width), lnx_w=lnx_w[l].reshape(1, width),
                  lnx_b=lnx_b[l].reshape(1, width))
        lam_vecs = jnp.stack([lam_q1[l], lam_k1[l], lam_q2[l], lam_k2[l]], axis=0)
        subw = subln_w[l].reshape(1, PAIR)
        last = l == depth - 1

        cols, rg, q, kb, vt, dg, kst, vst = _proj(xp, norm_w[l], w_l, bounds, tm=2 * tile, prompt=True, seq=seq,
                                                  layer=l, depth=depth, stacked=() if l == 0 else (kst, vst))
        cols_s, rg_s, q_s, k_s, v_s, dg_s = _proj(xs, norm_w[l], w_l, bounds, tm=tile, prompt=False)
        r3 = lambda t: t.reshape(bs, tpad, -1)

        rw, s_new, df_s = _rwkv_prompt_attn_sample(
            cols, rg, jnp.zeros((bp, ncols), F32), jnp.zeros((bp, heads_r, HEAD_DIM, HEAD_DIM), F32), lp,
            r3(q_s), r3(k_s), r3(v_s), r3(dg_s), subw, lam_vecs, cache_k, cache_v, page_table,
            nseq_total=bp, rows=chunk, nsub=4, layer=l, lam_init=lam_init)
        df = _attn_prompt(q, kb, vt, dg, subw, lam_vecs, batch=bp, seq=seq, tq=tile, lam_init=lam_init)
        xp = _outproj(xp, rw, df, wo_l, final_norm_w, final=last, tm=2 * tile)
        outs["sp"].append(s_new)
        outs["hp"].append(cols.reshape(bp, seq, ncols)[:, -1, :])

        prev = jnp.pad(state_shift[l][:, None, :], ((0, 0), (0, tpad - 1), (0, 0))).reshape(bs * tpad, ncols)
        rw_s, sst = _rwkv_sample(cols_s, rg_s, prev, state_rwkv[l], lp, nseq_total=bs, rows=chunk, tlen=tpad, nsub=1,
                                 t_valid=tdec, layer=l, depth=depth, stacked=() if l == 0 else (sst,))
        xs = _outproj(xs, rw_s, df_s.reshape(bs * tpad, dwidth), wo_l, final_norm_w, final=last, tm=tile)
        outs["ks"].append(r3(k_s)[:, :tdec].reshape(bs, tdec, heads_d, PAIR))
        outs["vs"].append(r3(v_s)[:, :tdec].reshape(bs, tdec, heads_d, PAIR))
        outs["hs"].append(r3(cols_s)[:, tdec - 1, :])

    y_prompt = xp.reshape(bp, seq, d)
    y_sample = xs.reshape(bs, tpad, d)[:, :tdec]
    st = lambda n: jnp.stack(outs[n], axis=0)
    k_prompt = kst.reshape(depth, bp, seq, heads_d, PAIR)
    v_prompt = vst.reshape(depth, bp, seq, heads_d, PAIR)
    return (y_prompt, y_sample, k_prompt, v_prompt, st("sp"), st("hp"), st("ks"), st("vs"), sst, st("hs"))
```
